```python
import math
import jax, jax.numpy as jnp
from jax import lax
import numpy as np

D_MODEL = 1024
BATCH = 2
SEQ = 8192
DEPTH = 2

RMS_EPS = 1e-6
CHUNK = 128
A_HEADS = 4
A_HEAD_DIM = D_MODEL // 8
A_WIDTH = A_HEADS * A_HEAD_DIM
B_GROUP = 16
B_WIDTH = D_MODEL // 2
B_GROUPS = B_WIDTH // B_GROUP
B_STATE = 64
DT_MIN = 1e-3
DT_MAX = 1e-1
AB_IN = 2 * A_WIDTH + B_WIDTH
AB_OUT = A_WIDTH + B_WIDTH
C_HEADS = 8
QK_NOPE = 128
QK_ROPE = 64
V_HEAD = 128
Q_LORA = 384
KV_LORA = 256
C_IN = Q_LORA + KV_LORA + QK_ROPE
ROPE_THETA = 10000.0
Q_BLOCK = 128
D_FF = 2816
N_EXPERTS = 8
TOP_K = 2
D_FF_EXPERT = 3584
MOE_BLOCK = 128

kernel_name = "hybrid_sgu_s5_mla_moe_encoder"


def rmsnorm(x, g):
    xf = x.astype(jnp.float32)
    y = xf * lax.rsqrt(jnp.mean(xf * xf, axis=-1, keepdims=True) + RMS_EPS)
    return (y * g.astype(jnp.float32)).astype(x.dtype)


def modulate(h, shift, scale):
    return h * (1.0 + scale[:, None, :]) + shift[:, None, :]


def chunked_sgu(u, v, v_g, w_s, b_s):
    bn, seq = u.shape[:2]
    v = rmsnorm(v, v_g)
    vc = v.reshape(bn, seq // CHUNK, CHUNK, A_HEADS, A_HEAD_DIM)
    s = jnp.einsum('gqk,bnkgc->bnqgc', w_s, vc) + b_s.T[None, None, :, :, None]
    return u * s.reshape(u.shape)


def s5_scan(u, lam_re, lam_im, log_dt, b_re, b_im, c_re, c_im, reverse):
    f32 = jnp.float32
    lam_re, lam_im = lam_re.astype(f32), lam_im.astype(f32)
    dt = jnp.exp(log_dt.astype(f32))[:, None]
    mag = jnp.exp(lam_re * dt)
    ang = lam_im * dt
    abar_re, abar_im = mag * jnp.cos(ang), mag * jnp.sin(ang)
    den = lam_re * lam_re + lam_im * lam_im
    fr = ((abar_re - 1.0) * lam_re + abar_im * lam_im) / den
    fi = (abar_im * lam_re - (abar_re - 1.0) * lam_im) / den
    b_re, b_im = b_re.astype(f32), b_im.astype(f32)
    bb_re = fr[:, :, None] * b_re - fi[:, :, None] * b_im
    bb_im = fr[:, :, None] * b_im + fi[:, :, None] * b_re
    bu_re = jnp.einsum('gph,blgh->blgp', bb_re, u)
    bu_im = jnp.einsum('gph,blgh->blgp', bb_im, u)
    a_re = jnp.broadcast_to(abar_re, bu_re.shape)
    a_im = jnp.broadcast_to(abar_im, bu_im.shape)

    def combine(e1, e2):
        a1r, a1i, b1r, b1i = e1
        a2r, a2i, b2r, b2i = e2
        return (a2r * a1r - a2i * a1i,
                a2r * a1i + a2i * a1r,
                a2r * b1r - a2i * b1i + b2r,
                a2r * b1i + a2i * b1r + b2i)

    _, _, x_re, x_im = lax.associative_scan(combine, (a_re, a_im, bu_re, bu_im),
                                            reverse=reverse, axis=1)
    return (jnp.einsum('ghp,blgp->blgh', c_re.astype(f32), x_re)
            - jnp.einsum('ghp,blgp->blgh', c_im.astype(f32), x_im))


def s5_mixer(u, lam_re, lam_im, log_dt, b_re, b_im, c_re, c_im, d_skip, w_glu):
    bn, seq, _ = u.shape
    uf = u.astype(jnp.float32)
    ug = uf.reshape(bn, seq, B_GROUPS, B_GROUP)
    y = (s5_scan(ug, lam_re[0], lam_im[0], log_dt[0], b_re[0], b_im[0], c_re[0], c_im[0], False)
         + s5_scan(ug, lam_re[1], lam_im[1], log_dt[1], b_re[1], b_im[1], c_re[1], c_im[1], True))
    y = y.reshape(bn, seq, B_WIDTH) + d_skip.astype(jnp.float32) * uf
    y = jax.nn.gelu(y).astype(u.dtype)
    return y * jax.nn.sigmoid(y @ w_glu)


def ab_mixer(h, w_in, v_g, w_s, b_s, lam_re, lam_im, log_dt, b_re, b_im, c_re, c_im,
             d_skip, w_glu, w_out):
    bn, seq, _ = h.shape
    z = h @ w_in
    za = jax.nn.gelu(z[..., :2 * A_WIDTH])
    u_a = za[..., :A_WIDTH].reshape(bn, seq, A_HEADS, A_HEAD_DIM)
    v_a = za[..., A_WIDTH:].reshape(bn, seq, A_HEADS, A_HEAD_DIM)
    a_out = chunked_sgu(u_a, v_a, v_g, w_s, b_s).reshape(bn, seq, A_WIDTH)
    b_out = s5_mixer(z[..., 2 * A_WIDTH:], lam_re, lam_im, log_dt, b_re, b_im, c_re, c_im,
                     d_skip, w_glu)
    return jnp.concatenate([a_out, b_out], axis=-1) @ w_out


def rope(x, cos, sin):
    xf = x.astype(jnp.float32)
    half = xf.shape[-1] // 2
    x1, x2 = xf[..., :half], xf[..., half:]
    return jnp.concatenate([x1 * cos - x2 * sin, x1 * sin + x2 * cos], axis=-1).astype(x.dtype)


def mla_mixer(h, w_in, q_norm_g, kv_norm_g, w_uq, w_ukv, w_o):
    bn, seq, _ = h.shape
    z = h @ w_in
    cq = rmsnorm(z[..., :Q_LORA], q_norm_g)
    ckv = rmsnorm(z[..., Q_LORA:Q_LORA + KV_LORA], kv_norm_g)
    k_rope = z[..., Q_LORA + KV_LORA:]
    q = (cq @ w_uq).reshape(bn, seq, C_HEADS, QK_NOPE + QK_ROPE)
    kv = (ckv @ w_ukv).reshape(bn, seq, C_HEADS, QK_NOPE + V_HEAD)
    q_nope, q_rope = q[..., :QK_NOPE], q[..., QK_NOPE:]
    k_nope, v = kv[..., :QK_NOPE], kv[..., QK_NOPE:]

    pos = jnp.arange(seq, dtype=jnp.float32)
    inv_freq = ROPE_THETA ** (-jnp.arange(0, QK_ROPE, 2, dtype=jnp.float32) / QK_ROPE)
    ang = pos[:, None] * inv_freq[None, :]
    cos, sin = jnp.cos(ang), jnp.sin(ang)
    q_rope = rope(q_rope, cos[None, :, None, :], sin[None, :, None, :])
    k_rope = rope(k_rope, cos[None], sin[None])

    scale = 1.0 / math.sqrt(QK_NOPE + QK_ROPE)
    nb = seq // Q_BLOCK
    qn_b = q_nope.reshape(bn, nb, Q_BLOCK, C_HEADS, QK_NOPE).transpose(1, 0, 2, 3, 4)
    qr_b = q_rope.reshape(bn, nb, Q_BLOCK, C_HEADS, QK_ROPE).transpose(1, 0, 2, 3, 4)

    def attend(blk):
        qn, qr = blk
        s = (jnp.einsum('bqhd,bkhd->bhqk', qn, k_nope)
             + jnp.einsum('bqhr,bkr->bhqk', qr, k_rope))
        p = jax.nn.softmax(s.astype(jnp.float32) * scale, axis=-1).astype(v.dtype)
        return jnp.einsum('bhqk,bkhd->bqhd', p, v)

    o = lax.map(attend, (qn_b, qr_b))
    o = o.transpose(1, 0, 2, 3, 4).reshape(bn, seq, C_HEADS * V_HEAD)
    return o @ w_o


def swiglu(h, w1, w3, w2):
    return (jax.nn.silu(h @ w1) * (h @ w3)) @ w2


def moe_swiglu(h, w_router, w1, w3, w2):
    bn, seq, d = h.shape
    t = bn * seq
    xf = h.reshape(t, d)
    logits = (xf @ w_router).astype(jnp.float32)
    top_val, top_idx = lax.top_k(logits, TOP_K)
    gates = jax.nn.softmax(top_val, axis=-1)
    e_flat = top_idx.reshape(-1).astype(jnp.int32)
    tok_flat = jnp.repeat(jnp.arange(t, dtype=jnp.int32), TOP_K)
    g_flat = gates.reshape(-1)
    order = jnp.argsort(e_flat, stable=True)
    e_sorted = e_flat[order]
    counts = jax.ops.segment_sum(jnp.ones_like(e_flat), e_flat, num_segments=N_EXPERTS)
    padded = ((counts + MOE_BLOCK - 1) // MOE_BLOCK) * MOE_BLOCK
    pad_end = jnp.cumsum(padded)
    pad_start = pad_end - padded
    start = jnp.cumsum(counts) - counts
    dest = pad_start[e_sorted] + (jnp.arange(t * TOP_K, dtype=jnp.int32) - start[e_sorted])
    n_rows = t * TOP_K + N_EXPERTS * MOE_BLOCK
    row_tok = jnp.full((n_rows,), t, jnp.int32).at[dest].set(tok_flat[order])
    row_gate = jnp.zeros((n_rows,), jnp.float32).at[dest].set(g_flat[order])
    nb = n_rows // MOE_BLOCK
    blk_start = jnp.arange(nb, dtype=jnp.int32) * MOE_BLOCK
    blk_expert = jnp.minimum(jnp.searchsorted(pad_end, blk_start, side='right'), N_EXPERTS - 1)
    x_pad = jnp.concatenate([xf, jnp.zeros((1, d), xf.dtype)], axis=0)
    xr = x_pad[row_tok].reshape(nb, MOE_BLOCK, d)

    def expert_block(args):
        xb, e = args
        return swiglu(xb, w1[e], w3[e], w2[e])

    yr = lax.map(expert_block, (xr, blk_expert)).reshape(n_rows, d)
    yr = yr * row_gate[:, None].astype(yr.dtype)
    y = jax.ops.segment_sum(yr, row_tok, num_segments=t + 1)[:t]
    return y.reshape(bn, seq, d)


def setup_inputs(seed: int = 0) -> dict:
    key = jax.random.key(seed)
    ks = iter(jax.random.split(key, 40))
    f32 = jnp.float32
    n_even = (DEPTH + 1) // 2
    n_odd = DEPTH // 2
    D = D_MODEL

    def nrm(shape, scale):
        return jax.random.normal(next(ks), shape, f32) * scale

    x = nrm((BATCH, SEQ, D), 1.0)
    c = nrm((BATCH, D), 1.0)
    gate_offset = jnp.repeat(jnp.array([0., 0., 1., 0., 0., 1.], f32), D)
    ada_w = nrm((DEPTH, D, 6 * D), 0.02)
    ada_b = nrm((DEPTH, 6 * D), 0.02) + gate_offset[None]
    norm_mix_g = 1.0 + nrm((DEPTH, D), 0.05)
    norm_ffn_g = 1.0 + nrm((DEPTH, D), 0.05)
    final_g = 1.0 + nrm((D,), 0.05)

    ab_w_in = nrm((n_even, D, AB_IN), D ** -0.5)
    sgu_v_g = 1.0 + nrm((n_even, A_HEADS, A_HEAD_DIM), 0.05)
    sgu_w_s = nrm((n_even, A_HEADS, CHUNK, CHUNK), CHUNK ** -0.5)
    sgu_b_s = 1.0 + nrm((n_even, A_HEADS, CHUNK), 0.1)
    s5_lam_re = -0.5 + nrm((n_even, 2, B_GROUPS, B_STATE), 0.01)
    s5_lam_im = (math.pi * jnp.arange(B_STATE, dtype=f32))[None, None, None, :] \
        + nrm((n_even, 2, B_GROUPS, B_STATE), 0.01)
    s5_log_dt = jax.random.uniform(next(ks), (n_even, 2, B_GROUPS), f32,
                                   math.log(DT_MIN), math.log(DT_MAX))
    s5_b_re = nrm((n_even, 2, B_GROUPS, B_STATE, B_GROUP), (2 * B_GROUP) ** -0.5)
    s5_b_im = nrm((n_even, 2, B_GROUPS, B_STATE, B_GROUP), (2 * B_GROUP) ** -0.5)
    s5_c_re = nrm((n_even, 2, B_GROUPS, B_GROUP, B_STATE), 0.5)
    s5_c_im = nrm((n_even, 2, B_GROUPS, B_GROUP, B_STATE), 0.5)
    s5_d = nrm((n_even, B_WIDTH), 1.0)
    s5_w_glu = nrm((n_even, B_WIDTH, B_WIDTH), B_WIDTH ** -0.5)
    ab_w_out = nrm((n_even, AB_OUT, D), AB_OUT ** -0.5)

    ffn_w1 = nrm((n_even, D, D_FF), D ** -0.5)
    ffn_w3 = nrm((n_even, D, D_FF), D ** -0.5)
    ffn_w2 = nrm((n_even, D_FF, D), D_FF ** -0.5)

    mla_w_in = nrm((n_odd, D, C_IN), D ** -0.5)
    mla_q_norm_g = 1.0 + nrm((n_odd, Q_LORA), 0.05)
    mla_kv_norm_g = 1.0 + nrm((n_odd, KV_LORA), 0.05)
    mla_w_uq = nrm((n_odd, Q_LORA, C_HEADS * (QK_NOPE + QK_ROPE)), Q_LORA ** -0.5)
    mla_w_ukv = nrm((n_odd, KV_LORA, C_HEADS * (QK_NOPE + V_HEAD)), KV_LORA ** -0.5)
    mla_w_o = nrm((n_odd, C_HEADS * V_HEAD, D), (C_HEADS * V_HEAD) ** -0.5)

    moe_w_router = nrm((n_odd, D, N_EXPERTS), D ** -0.5)
    moe_w1 = nrm((n_odd, N_EXPERTS, D, D_FF_EXPERT), D ** -0.5)
    moe_w3 = nrm((n_odd, N_EXPERTS, D, D_FF_EXPERT), D ** -0.5)
    moe_w2 = nrm((n_odd, N_EXPERTS, D_FF_EXPERT, D), D_FF_EXPERT ** -0.5)

    return {"x": x, "c": c, "ada_w": ada_w, "ada_b": ada_b,
            "norm_mix_g": norm_mix_g, "norm_ffn_g": norm_ffn_g, "final_g": final_g,
            "ab_w_in": ab_w_in, "sgu_v_g": sgu_v_g, "sgu_w_s": sgu_w_s, "sgu_b_s": sgu_b_s,
            "s5_lam_re": s5_lam_re, "s5_lam_im": s5_lam_im, "s5_log_dt": s5_log_dt,
            "s5_b_re": s5_b_re, "s5_b_im": s5_b_im, "s5_c_re": s5_c_re, "s5_c_im": s5_c_im,
            "s5_d": s5_d, "s5_w_glu": s5_w_glu, "ab_w_out": ab_w_out,
            "ffn_w1": ffn_w1, "ffn_w3": ffn_w3, "ffn_w2": ffn_w2,
            "mla_w_in": mla_w_in, "mla_q_norm_g": mla_q_norm_g, "mla_kv_norm_g": mla_kv_norm_g,
            "mla_w_uq": mla_w_uq, "mla_w_ukv": mla_w_ukv, "mla_w_o": mla_w_o,
            "moe_w_router": moe_w_router, "moe_w1": moe_w1, "moe_w3": moe_w3, "moe_w2": moe_w2}


def reference(x, c, ada_w, ada_b, norm_mix_g, norm_ffn_g, final_g,
              ab_w_in, sgu_v_g, sgu_w_s, sgu_b_s,
              s5_lam_re, s5_lam_im, s5_log_dt, s5_b_re, s5_b_im, s5_c_re, s5_c_im,
              s5_d, s5_w_glu, ab_w_out,
              ffn_w1, ffn_w3, ffn_w2,
              mla_w_in, mla_q_norm_g, mla_kv_norm_g, mla_w_uq, mla_w_ukv, mla_w_o,
              moe_w_router, moe_w1, moe_w3, moe_w2):
    c_act = jax.nn.silu(c)
    for layer in range(DEPTH):
        i = layer // 2
        mod = c_act @ ada_w[layer] + ada_b[layer]
        sh_m, sc_m, g_m, sh_f, sc_f, g_f = jnp.split(mod, 6, axis=-1)

        h = modulate(rmsnorm(x, norm_mix_g[layer]), sh_m, sc_m)
        if layer % 2 == 0:
            y = ab_mixer(h, ab_w_in[i], sgu_v_g[i], sgu_w_s[i], sgu_b_s[i],
                         s5_lam_re[i], s5_lam_im[i], s5_log_dt[i], s5_b_re[i], s5_b_im[i],
                         s5_c_re[i], s5_c_im[i], s5_d[i], s5_w_glu[i], ab_w_out[i])
        else:
            y = mla_mixer(h, mla_w_in[i], mla_q_norm_g[i], mla_kv_norm_g[i],
                          mla_w_uq[i], mla_w_ukv[i], mla_w_o[i])
        x = x + g_m[:, None, :] * y

        h = modulate(rmsnorm(x, norm_ffn_g[layer]), sh_f, sc_f)
        if layer % 2 == 0:
            y = swiglu(h, ffn_w1[i], ffn_w3[i], ffn_w2[i])
        else:
            y = moe_swiglu(h, moe_w_router[i], moe_w1[i], moe_w3[i], moe_w2[i])
        x = x + g_f[:, None, :] * y
    return rmsnorm(x, final_g)
```

```python
import functools
import math

import jax
import jax.numpy as jnp
from jax import lax
from jax.experimental import pallas as pl
from jax.experimental.pallas import tpu as pltpu

F32 = jnp.float32
BF16 = jnp.bfloat16
HIGHEST = lax.Precision.HIGHEST

LANES = 128
SUBLANES = 8
MIB = 1024 * 1024

RMS_EPS = 1e-6
C_HEADS = 8
QK_NOPE = 128
QK_ROPE = 64
V_HEAD = 128
ROPE_THETA = 10000.0
TOP_K = 2

S5_Q = 8
MOE_PAD = 512


def _cparams(semantics, vmem_mib):
    return pltpu.CompilerParams(dimension_semantics=semantics, vmem_limit_bytes=vmem_mib * MIB)


def _dot(a, b):
    return jnp.dot(a, b, preferred_element_type=F32)


def _norm_mod(x, g, shift, scale):
    y = x * lax.rsqrt(jnp.mean(x * x, axis=-1, keepdims=True) + RMS_EPS)
    return (y * g) * (1.0 + scale) + shift


def _row_block(tm, seq):
    per_batch = seq // tm
    return lambda i, *_: (i // per_batch, 0, 0)


def _adaln_kernel(c_ref, w_ref, b_ref, o_ref):
    c = c_ref[...]
    ca = c * jax.nn.sigmoid(c)
    o_ref[0] = jnp.dot(ca, w_ref[0], preferred_element_type=F32, precision=HIGHEST) + b_ref[0]


def _adaln(c, ada_w, ada_b):
    depth, d, n = ada_w.shape
    bn = c.shape[0]
    tn = n // 4
    c8 = jnp.zeros((SUBLANES, d), F32).at[:bn].set(c)
    out = pl.pallas_call(
        _adaln_kernel,
        grid=(depth, n // tn),
        in_specs=[
            pl.BlockSpec((SUBLANES, d), lambda l, j: (0, 0)),
            pl.BlockSpec((1, d, tn), lambda l, j: (l, 0, j)),
            pl.BlockSpec((1, 1, tn), lambda l, j: (l, 0, j)),
        ],
        out_specs=pl.BlockSpec((1, SUBLANES, tn), lambda l, j: (l, 0, j)),
        out_shape=jax.ShapeDtypeStruct((depth, SUBLANES, n), F32),
        compiler_params=_cparams(("parallel", "parallel"), 32),
        name="adaln",
    )(c8, ada_w, ada_b.reshape(depth, 1, n))
    return out[:, :bn].reshape(depth, bn, 6, d)


def _l0_in_kernel(x_ref, mod_ref, g_ref, win_ref, ws_ref, bs_ref, vg_ref, a_ref, u_ref, *,
                  heads, chunk, n_tiles):
    x = x_ref[...]
    h = _norm_mod(x, g_ref[...], mod_ref[0, 0:1, :], mod_ref[0, 1:2, :]).astype(BF16)
    z = _dot(h, win_ref[...])
    tm = x.shape[0]
    a_width = heads * LANES
    for g in range(heads):
        u = jax.nn.gelu(z[:, g * LANES:(g + 1) * LANES])
        v = jax.nn.gelu(z[:, a_width + g * LANES:a_width + (g + 1) * LANES])
        vn = v * lax.rsqrt(jnp.mean(v * v, axis=-1, keepdims=True) + RMS_EPS) * vg_ref[g:g + 1, :]
        vnb = vn.astype(BF16)
        bias = bs_ref[:, g:g + 1]
        for ci in range(tm // chunk):
            rows = slice(ci * chunk, (ci + 1) * chunk)
            s = _dot(ws_ref[g], vnb[rows]) + bias
            a_ref[rows, g * LANES:(g + 1) * LANES] = (u[rows] * s).astype(BF16)
    for j in range(n_tiles):
        u_ref[j] = z[:, 2 * a_width + j * LANES:2 * a_width + (j + 1) * LANES].astype(BF16)


def _l0_in(x2, mod, norm_g, w_in, w_s, b_s, v_g, seq, tm):
    t, d = x2.shape
    heads, chunk, _ = w_s.shape
    a_width = heads * LANES
    b_width = w_in.shape[1] - 2 * a_width
    n_tiles = b_width // LANES
    kern = functools.partial(_l0_in_kernel, heads=heads, chunk=chunk, n_tiles=n_tiles)
    return pl.pallas_call(
        kern,
        grid=(t // tm,),
        in_specs=[
            pl.BlockSpec((tm, d), lambda i: (i, 0)),
            pl.BlockSpec((1, 6, d), _row_block(tm, seq)),
            pl.BlockSpec((1, d), lambda i: (0, 0)),
            pl.BlockSpec(w_in.shape, lambda i: (0, 0)),
            pl.BlockSpec(w_s.shape, lambda i: (0, 0, 0)),
            pl.BlockSpec((chunk, heads), lambda i: (0, 0)),
            pl.BlockSpec((heads, LANES), lambda i: (0, 0)),
        ],
        out_specs=[
            pl.BlockSpec((tm, a_width), lambda i: (i, 0)),
            pl.BlockSpec((n_tiles, tm, LANES), lambda i: (0, i, 0)),
        ],
        out_shape=[
            jax.ShapeDtypeStruct((t, a_width), BF16),
            jax.ShapeDtypeStruct((n_tiles, t, LANES), BF16),
        ],
        compiler_params=_cparams(("parallel",), 40),
        name="l0_in_sgu",
    )(x2, mod, norm_g.reshape(1, d), w_in.astype(BF16), w_s.astype(BF16), b_s.T, v_g)


def _s5_weights(lam_re, lam_im, log_dt, b_re, b_im, c_re, c_im, q):
    _, groups, p = lam_re.shape
    h = b_re.shape[-1]
    gpt = LANES // h
    nt = groups // gpt
    dt = jnp.exp(log_dt.astype(F32))[..., None]
    mag = jnp.exp(lam_re * dt)
    ang = lam_im * dt
    ar, ai = mag * jnp.cos(ang), mag * jnp.sin(ang)
    den = lam_re * lam_re + lam_im * lam_im
    fr = ((ar - 1.0) * lam_re + ai * lam_im) / den
    fi = (ai * lam_re - (ar - 1.0) * lam_im) / den
    bbr = fr[..., None] * b_re - fi[..., None] * b_im
    bbi = fr[..., None] * b_im + fi[..., None] * b_re
    prs, pis = [jnp.ones_like(ar)], [jnp.zeros_like(ar)]
    for _ in range(q):
        prs.append(prs[-1] * ar - pis[-1] * ai)
        pis.append(prs[-2] * ai + pis[-1] * ar)
    pr, pi = jnp.stack(prs), jnp.stack(pis)
    abr = pr[..., None] * bbr - pi[..., None] * bbi
    abi = pr[..., None] * bbi + pi[..., None] * bbr
    car = c_re * pr[:, :, :, None, :] - c_im * pi[:, :, :, None, :]
    cai = c_re * pi[:, :, :, None, :] + c_im * pr[:, :, :, None, :]
    kk = (jnp.einsum('dgop,kdgpi->kdgoi', c_re, abr[:q], precision=HIGHEST)
          - jnp.einsum('dgop,kdgpi->kdgoi', c_im, abi[:q], precision=HIGHEST))
    eye = jnp.eye(gpt, dtype=F32)
    idx = jnp.arange(q)
    dfi = idx[None, :] - idx[:, None]
    kf = jnp.where((dfi >= 0)[:, :, None, None, None], kk[:, 0][jnp.clip(dfi, 0, q - 1)], 0.0)
    kb = jnp.where((dfi <= 0)[:, :, None, None, None], kk[:, 1][jnp.clip(-dfi, 0, q - 1)], 0.0)
    m = (kf + kb).reshape(q, q, nt, gpt, h, h)
    t_mat = jnp.einsum('sijgoa,gh->jsgaiho', m, eye).reshape(nt, q * LANES, q * LANES)

    w4 = jnp.stack([jnp.stack([abr[q - 1 - idx, 0], abi[q - 1 - idx, 0]]),
                    jnp.stack([abr[idx, 1], abi[idx, 1]])])
    w4 = w4.reshape(2, 2, q, nt, gpt, p, h)
    ws = jnp.einsum('drsjgpa,gh->jsgadrhp', w4, eye).reshape(nt, q * LANES, 4 * gpt * p)

    y4 = jnp.stack([jnp.stack([car[idx + 1, 0], -cai[idx + 1, 0]]),
                    jnp.stack([car[q - idx, 1], -cai[q - idx, 1]])])
    y4 = y4.reshape(2, 2, q, nt, gpt, h, p)
    wy = jnp.einsum('drijgop,gh->jdrgpiho', y4, eye).reshape(nt, 4 * gpt * p, q * LANES)

    aq = jnp.stack([jnp.stack([pr[q, 0], pi[q, 0]]), jnp.stack([pr[q, 1], pi[q, 1]])])
    aq = aq.reshape(2, 2, nt, gpt, p).transpose(2, 0, 1, 3, 4).reshape(nt, 1, 4 * gpt * p)
    return t_mat.astype(BF16), ws.astype(BF16), wy.astype(BF16), aq


def _s5_kernel(u_ref, t_ref, ws_ref, wy_ref, aq_ref, y_ref, s_ref):
    u = u_ref[0]
    s_ref[...] = _dot(u, ws_ref[0])
    rows, width = s_ref.shape
    hw = width // 4
    aq = aq_ref[0]
    afr, afi = aq[:, 0:hw], aq[:, hw:2 * hw]
    abr, abi = aq[:, 2 * hw:3 * hw], aq[:, 3 * hw:4 * hw]

    def body(c, carry):
        xfr, xfi, xbr, xbi = carry
        rf = pl.ds(c, 1)
        rb = pl.ds(rows - 1 - c, 1)
        sfr = s_ref[rf, 0:hw]
        sfi = s_ref[rf, hw:2 * hw]
        sbr = s_ref[rb, 2 * hw:3 * hw]
        sbi = s_ref[rb, 3 * hw:4 * hw]
        s_ref[rf, 0:hw] = xfr
        s_ref[rf, hw:2 * hw] = xfi
        s_ref[rb, 2 * hw:3 * hw] = xbr
        s_ref[rb, 3 * hw:4 * hw] = xbi
        return (afr * xfr - afi * xfi + sfr, afr * xfi + afi * xfr + sfi,
                abr * xbr - abi * xbi + sbr, abr * xbi + abi * xbr + sbi)

    zero = jnp.zeros((1, hw), F32)
    lax.fori_loop(0, rows, body, (zero, zero, zero, zero), unroll=8)
    y_ref[0] = _dot(u, t_ref[0]) + _dot(s_ref[...].astype(BF16), wy_ref[0])


def _s5(u_tiles, t_mat, ws, wy, aq, bn, q):
    nt, t, _ = u_tiles.shape
    rows = t // q // bn
    u2 = u_tiles.reshape(nt, t // q, q * LANES)
    sw = ws.shape[2]
    y2 = pl.pallas_call(
        _s5_kernel,
        grid=(nt, bn),
        in_specs=[
            pl.BlockSpec((1, rows, q * LANES), lambda j, b: (j, b, 0)),
            pl.BlockSpec((1,) + t_mat.shape[1:], lambda j, b: (j, 0, 0)),
            pl.BlockSpec((1,) + ws.shape[1:], lambda j, b: (j, 0, 0)),
            pl.BlockSpec((1,) + wy.shape[1:], lambda j, b: (j, 0, 0)),
            pl.BlockSpec((1, 1, sw), lambda j, b: (j, 0, 0)),
        ],
        out_specs=pl.BlockSpec((1, rows, q * LANES), lambda j, b: (j, b, 0)),
        out_shape=jax.ShapeDtypeStruct((nt, t // q, q * LANES), F32),
        scratch_shapes=[pltpu.VMEM((rows, sw), F32)],
        compiler_params=_cparams(("parallel", "parallel"), 48),
        name="s5_chunked",
    )(u2, t_mat, ws, wy, aq)
    return y2.reshape(nt, t, LANES)


def _l0_out_kernel(x_ref, mod_ref, a_ref, y5_ref, u_ref, d_ref, wglu_ref, wout_ref, o_ref, *, n_tiles):
    ys = []
    for j in range(n_tiles):
        ys.append(jax.nn.gelu(y5_ref[j] + d_ref[j] * u_ref[j].astype(F32)))
    y = jnp.concatenate(ys, axis=-1)
    glu = jax.nn.sigmoid(_dot(y.astype(BF16), wglu_ref[...]))
    b_out = (y * glu).astype(BF16)
    a_width = a_ref.shape[1]
    out = _dot(a_ref[...], wout_ref[0:a_width, :]) + _dot(b_out, wout_ref[a_width:, :])
    o_ref[...] = x_ref[...] + mod_ref[0, 2:3, :] * out


def _l0_out(x2, mod, a_out, y5, u_tiles, d_skip, w_glu, w_out, seq, tm):
    t, d = x2.shape
    n_tiles = u_tiles.shape[0]
    a_width = a_out.shape[1]
    kern = functools.partial(_l0_out_kernel, n_tiles=n_tiles)
    return pl.pallas_call(
        kern,
        grid=(t // tm,),
        in_specs=[
            pl.BlockSpec((tm, d), lambda i: (i, 0)),
            pl.BlockSpec((1, 6, d), _row_block(tm, seq)),
            pl.BlockSpec((tm, a_width), lambda i: (i, 0)),
            pl.BlockSpec((n_tiles, tm, LANES), lambda i: (0, i, 0)),
            pl.BlockSpec((n_tiles, tm, LANES), lambda i: (0, i, 0)),
            pl.BlockSpec((n_tiles, 1, LANES), lambda i: (0, 0, 0)),
            pl.BlockSpec(w_glu.shape, lambda i: (0, 0)),
            pl.BlockSpec(w_out.shape, lambda i: (0, 0)),
        ],
        out_specs=pl.BlockSpec((tm, d), lambda i: (i, 0)),
        out_shape=jax.ShapeDtypeStruct((t, d), F32),
        compiler_params=_cparams(("parallel",), 40),
        name="l0_out",
    )(x2, mod, a_out, y5, u_tiles, d_skip.reshape(n_tiles, 1, LANES),
      w_glu.astype(BF16), w_out.astype(BF16))


def _ffn_kernel(x_ref, mod_ref, g_ref, w1_ref, w3_ref, w2_ref, o_ref, h_scr, acc_scr):
    f = pl.program_id(1)

    @pl.when(f == 0)
    def _():
        h_scr[...] = _norm_mod(x_ref[...], g_ref[...], mod_ref[0, 3:4, :], mod_ref[0, 4:5, :]).astype(BF16)
        acc_scr[...] = jnp.zeros_like(acc_scr)

    h = h_scr[...]
    a = _dot(h, w1_ref[...])
    b = _dot(h, w3_ref[...])
    act = (a * jax.nn.sigmoid(a) * b).astype(BF16)
    acc_scr[...] += _dot(act, w2_ref[...])

    @pl.when(f == pl.num_programs(1) - 1)
    def _():
        o_ref[...] = x_ref[...] + mod_ref[0, 5:6, :] * acc_scr[...]


def _ffn(x2, mod, norm_g, w1, w3, w2, seq, tm, tf):
    t, d = x2.shape
    ff = w1.shape[1]
    return pl.pallas_call(
        _ffn_kernel,
        grid=(t // tm, ff // tf),
        in_specs=[
            pl.BlockSpec((tm, d), lambda i, f: (i, 0)),
            pl.BlockSpec((1, 6, d), _row_block(tm, seq)),
            pl.BlockSpec((1, d), lambda i, f: (0, 0)),
            pl.BlockSpec((d, tf), lambda i, f: (0, f)),
            pl.BlockSpec((d, tf), lambda i, f: (0, f)),
            pl.BlockSpec((tf, d), lambda i, f: (f, 0)),
        ],
        out_specs=pl.BlockSpec((tm, d), lambda i, f: (i, 0)),
        out_shape=jax.ShapeDtypeStruct((t, d), F32),
        scratch_shapes=[pltpu.VMEM((tm, d), BF16), pltpu.VMEM((tm, d), F32)],
        compiler_params=_cparams(("parallel", "arbitrary"), 48),
        name="ffn_swiglu",
    )(x2, mod, norm_g.reshape(1, d), w1.astype(BF16), w3.astype(BF16), w2.astype(BF16))


def _mla_proj_kernel(x_ref, mod_ref, g_ref, win_ref, gq_ref, gkv_ref, wq_ref, wkv_ref, cos_ref, sin_ref,
                     q_ref, k_ref, v_ref, *, q_lora, kv_lora, heads):
    h = _norm_mod(x_ref[...], g_ref[...], mod_ref[0, 0:1, :], mod_ref[0, 1:2, :]).astype(BF16)
    z = _dot(h, win_ref[...])
    cq = z[:, :q_lora]
    cq = (cq * lax.rsqrt(jnp.mean(cq * cq, axis=-1, keepdims=True) + RMS_EPS) * gq_ref[...]).astype(BF16)
    ckv = z[:, q_lora:q_lora + kv_lora]
    ckv = (ckv * lax.rsqrt(jnp.mean(ckv * ckv, axis=-1, keepdims=True) + RMS_EPS) * gkv_ref[...]).astype(BF16)
    cos = cos_ref[...]
    sin = sin_ref[...]
    r0 = q_lora + kv_lora
    k_rope = (z[:, r0:r0 + LANES] * cos + z[:, r0 + LANES:r0 + 2 * LANES] * sin).astype(BF16)
    qa = _dot(cq, wq_ref[...])
    kv = _dot(ckv, wkv_ref[...])
    hw = heads * LANES
    for hd in range(heads):
        c = slice(hd * LANES, (hd + 1) * LANES)
        q_ref[:, 2 * hd * LANES:(2 * hd + 1) * LANES] = qa[:, c].astype(BF16)
        q_rope = qa[:, hw + hd * LANES:hw + (hd + 1) * LANES] * cos \
            + qa[:, 2 * hw + hd * LANES:2 * hw + (hd + 1) * LANES] * sin
        q_ref[:, (2 * hd + 1) * LANES:(2 * hd + 2) * LANES] = q_rope.astype(BF16)
        k_ref[:, 2 * hd * LANES:(2 * hd + 1) * LANES] = kv[:, c].astype(BF16)
        k_ref[:, (2 * hd + 1) * LANES:(2 * hd + 2) * LANES] = k_rope
    v_ref[...] = kv[:, hw:].astype(BF16)


def _rope_pad(w_rope):
    half = QK_ROPE // 2
    x1, x2 = w_rope[..., :half], w_rope[..., half:]
    zeros = jnp.zeros(w_rope.shape[:-1] + (LANES - QK_ROPE,), w_rope.dtype)
    return jnp.concatenate([x1, x2, zeros], -1), jnp.concatenate([-x2, x1, zeros], -1)


def _mla_proj(x2, mod, norm_g, w_in, gq, gkv, w_uq, w_ukv, seq, tm):
    t, d = x2.shape
    q_lora, kv_lora = gq.shape[0], gkv.shape[0]
    heads = C_HEADS
    scale = 1.0 / math.sqrt(QK_NOPE + QK_ROPE)
    kr, krs = _rope_pad(w_in[:, q_lora + kv_lora:])
    w_in_ext = jnp.concatenate([w_in[:, :q_lora + kv_lora], kr, krs], -1).astype(BF16)
    wq3 = (w_uq * scale).reshape(q_lora, heads, QK_NOPE + QK_ROPE)
    qr, qrs = _rope_pad(wq3[..., QK_NOPE:])
    wq = jnp.concatenate([wq3[..., :QK_NOPE].reshape(q_lora, -1), qr.reshape(q_lora, -1),
                          qrs.reshape(q_lora, -1)], -1).astype(BF16)
    wkv3 = w_ukv.reshape(kv_lora, heads, QK_NOPE + V_HEAD)
    wkv = jnp.concatenate([wkv3[..., :QK_NOPE].reshape(kv_lora, -1),
                           wkv3[..., QK_NOPE:].reshape(kv_lora, -1)], -1).astype(BF16)
    half = QK_ROPE // 2
    inv_freq = ROPE_THETA ** (-jnp.arange(0, QK_ROPE, 2, dtype=F32) / QK_ROPE)
    ang = jnp.arange(seq, dtype=F32)[:, None] * inv_freq[None, :]
    pad = jnp.zeros((seq, LANES - QK_ROPE), F32)
    cos = jnp.concatenate([jnp.cos(ang), jnp.cos(ang), pad], -1)
    sin = jnp.concatenate([jnp.sin(ang), jnp.sin(ang), pad], -1)
    per_batch = seq // tm
    kern = functools.partial(_mla_proj_kernel, q_lora=q_lora, kv_lora=kv_lora, heads=heads)
    const = lambda i: (0, 0)
    return pl.pallas_call(
        kern,
        grid=(t // tm,),
        in_specs=[
            pl.BlockSpec((tm, d), lambda i: (i, 0)),
            pl.BlockSpec((1, 6, d), _row_block(tm, seq)),
            pl.BlockSpec((1, d), const),
            pl.BlockSpec(w_in_ext.shape, const),
            pl.BlockSpec((1, q_lora), const),
            pl.BlockSpec((1, kv_lora), const),
            pl.BlockSpec(wq.shape, const),
            pl.BlockSpec(wkv.shape, const),
            pl.BlockSpec((tm, LANES), lambda i: (i % per_batch, 0)),
            pl.BlockSpec((tm, LANES), lambda i: (i % per_batch, 0)),
        ],
        out_specs=[
            pl.BlockSpec((tm, 2 * heads * LANES), lambda i: (i, 0)),
            pl.BlockSpec((tm, 2 * heads * LANES), lambda i: (i, 0)),
            pl.BlockSpec((tm, heads * V_HEAD), lambda i: (i, 0)),
        ],
        out_shape=[
            jax.ShapeDtypeStruct((t, 2 * heads * LANES), BF16),
            jax.ShapeDtypeStruct((t, 2 * heads * LANES), BF16),
            jax.ShapeDtypeStruct((t, heads * V_HEAD), BF16),
        ],
        compiler_params=_cparams(("parallel",), 48),
        name="mla_proj",
    )(x2, mod, norm_g.reshape(1, d), w_in_ext, gq.reshape(1, -1), gkv.reshape(1, -1), wq, wkv, cos, sin)


def _attn_kernel(q_ref, k_ref, v_ref, o_ref, *, tk):
    q = q_ref[...]
    tq = q.shape[0]
    n_kv = k_ref.shape[0] // tk

    def body(c, carry):
        m, l, acc = carry
        start = pl.multiple_of(c * tk, tk)
        k = k_ref[pl.ds(start, tk), :]
        v = v_ref[pl.ds(start, tk), :]
        s = lax.dot_general(q, k, (((1,), (1,)), ((), ())), preferred_element_type=F32)
        m_new = jnp.maximum(m, jnp.max(s, axis=-1, keepdims=True))
        alpha = jnp.exp(m - m_new)
        p = jnp.exp(s - m_new)
        l = alpha * l + jnp.sum(p, axis=-1, keepdims=True)
        acc = alpha * acc + _dot(p.astype(BF16), v)
        return m_new, l, acc

    init = (jnp.full((tq, 1), -jnp.inf, F32), jnp.zeros((tq, 1), F32), jnp.zeros((tq, v_ref.shape[1]), F32))
    _, l, acc = lax.fori_loop(0, n_kv, body, init)
    o_ref[...] = (acc / l).astype(BF16)


def _attention(q, k, v, bn, seq, tq, tk):
    t = q.shape[0]
    heads = C_HEADS
    nq = seq // tq
    kern = functools.partial(_attn_kernel, tk=tk)
    return pl.pallas_call(
        kern,
        grid=(bn, heads, nq),
        in_specs=[
            pl.BlockSpec((tq, 2 * LANES), lambda b, h, i: (b * nq + i, h)),
            pl.BlockSpec((seq, 2 * LANES), lambda b, h, i: (b, h)),
            pl.BlockSpec((seq, V_HEAD), lambda b, h, i: (b, h)),
        ],
        out_specs=pl.BlockSpec((tq, V_HEAD), lambda b, h, i: (b * nq + i, h)),
        out_shape=jax.ShapeDtypeStruct((t, heads * V_HEAD), BF16),
        compiler_params=_cparams(("parallel", "parallel", "parallel"), 48),
        name="mla_attention",
    )(q, k, v)


def _attn_out_kernel(x_ref, mod_ref, o_ref, wo_ref, g_ref, wr_ref, x3_ref, h_ref, info_ref, cnt_ref, run_scr,
                     *, n_experts):
    i = pl.program_id(0)

    @pl.when(i == 0)
    def _():
        run_scr[...] = jnp.zeros_like(run_scr)

    x3 = x_ref[...] + mod_ref[0, 2:3, :] * _dot(o_ref[...], wo_ref[...])
    x3_ref[...] = x3
    h = _norm_mod(x3, g_ref[...], mod_ref[0, 3:4, :], mod_ref[0, 4:5, :])
    tm, d = h.shape
    for c in range(d // LANES):
        h_ref[pl.ds(c, tm, stride=SUBLANES), :] = h[:, c * LANES:(c + 1) * LANES]

    logits = jnp.dot(h, wr_ref[...], preferred_element_type=F32, precision=HIGHEST)
    lane = lax.broadcasted_iota(jnp.int32, (tm, LANES), 1).astype(F32)
    neg = jnp.float32(-jnp.inf)
    lg = jnp.where(lane < n_experts, logits, neg)
    m1 = jnp.max(lg, axis=-1, keepdims=True)
    i1 = jnp.min(jnp.where(lg == m1, lane, float(LANES)), axis=-1, keepdims=True)
    lg2 = jnp.where(lane == i1, neg, lg)
    m2 = jnp.max(lg2, axis=-1, keepdims=True)
    i2 = jnp.min(jnp.where(lg2 == m2, lane, float(LANES)), axis=-1, keepdims=True)
    e = jnp.exp(m2 - m1)
    g0 = 1.0 / (1.0 + e)
    g1 = e / (1.0 + e)
    sel1 = lane == i1
    sel2 = lane == i2
    onehot = jnp.where(sel1, 1.0, 0.0) + jnp.where(sel2, 1.0, 0.0)
    row = lax.broadcasted_iota(jnp.int32, (tm, tm), 0)
    col = lax.broadcasted_iota(jnp.int32, (tm, tm), 1)
    tri = jnp.where(col < row, 1.0, 0.0).astype(BF16)
    before = _dot(tri, onehot.astype(BF16)) + run_scr[...]
    r0 = jnp.sum(jnp.where(sel1, before, 0.0), axis=-1, keepdims=True)
    r1 = jnp.sum(jnp.where(sel2, before, 0.0), axis=-1, keepdims=True)
    run_scr[...] += jnp.sum(onehot, axis=0, keepdims=True)
    info = jnp.where(lane == 0, i1, jnp.where(lane == 1, i2, jnp.where(lane == 2, r0, jnp.where(
        lane == 3, r1, jnp.where(lane == 4, g0, jnp.where(lane == 5, g1, 0.0))))))
    info_ref[...] = info
    cnt_ref[...] = jnp.broadcast_to(run_scr[...], cnt_ref.shape)


def _attn_out(x2, mod, o, w_o, norm_g, w_router, seq, tm):
    t, d = x2.shape
    n_experts = w_router.shape[1]
    wr = jnp.zeros((d, LANES), F32).at[:, :n_experts].set(w_router)
    kern = functools.partial(_attn_out_kernel, n_experts=n_experts)
    const = lambda i: (0, 0)
    return pl.pallas_call(
        kern,
        grid=(t // tm,),
        in_specs=[
            pl.BlockSpec((tm, d), lambda i: (i, 0)),
            pl.BlockSpec((1, 6, d), _row_block(tm, seq)),
            pl.BlockSpec((tm, o.shape[1]), lambda i: (i, 0)),
            pl.BlockSpec(w_o.shape, const),
            pl.BlockSpec((1, d), const),
            pl.BlockSpec((d, LANES), const),
        ],
        out_specs=[
            pl.BlockSpec((tm, d), lambda i: (i, 0)),
            pl.BlockSpec((tm * SUBLANES, LANES), lambda i: (i, 0)),
            pl.BlockSpec((tm, LANES), lambda i: (i, 0)),
            pl.BlockSpec((SUBLANES, LANES), const),
        ],
        out_shape=[
            jax.ShapeDtypeStruct((t, d), F32),
            jax.ShapeDtypeStruct((t * SUBLANES, LANES), F32),
            jax.ShapeDtypeStruct((t, LANES), F32),
            jax.ShapeDtypeStruct((SUBLANES, LANES), F32),
        ],
        scratch_shapes=[pltpu.VMEM((1, LANES), F32)],
        compiler_params=_cparams(("arbitrary",), 40),
        name="attn_out_router",
    )(x2, mod, o, w_o.astype(BF16), norm_g.reshape(1, d), wr)


def _gather_kernel(tok_ref, h_hbm, o_hbm, sem, *, rows):
    base = pl.program_id(0) * rows

    def issue(r, carry):
        pltpu.make_async_copy(h_hbm.at[tok_ref[base + r]], o_hbm.at[base + r], sem).start()
        return carry

    lax.fori_loop(0, rows, issue, 0)

    def wait(r, carry):
        pltpu.make_async_copy(h_hbm.at[0], o_hbm.at[base], sem).wait()
        return carry

    lax.fori_loop(0, rows, wait, 0)


def _gather_rows(row_tok, h3, rows):
    n_rows = row_tok.shape[0]
    kern = functools.partial(_gather_kernel, rows=rows)
    return pl.pallas_call(
        kern,
        grid_spec=pltpu.PrefetchScalarGridSpec(
            num_scalar_prefetch=1,
            grid=(n_rows // rows,),
            in_specs=[pl.BlockSpec(memory_space=pl.ANY)],
            out_specs=pl.BlockSpec(memory_space=pl.ANY),
            scratch_shapes=[pltpu.SemaphoreType.DMA(())],
        ),
        out_shape=jax.ShapeDtypeStruct((n_rows,) + h3.shape[1:], h3.dtype),
        compiler_params=_cparams(("arbitrary",), 16),
        name="moe_gather",
    )(row_tok, h3)


def _experts_kernel(be_ref, na_ref, x_ref, w1_ref, w3_ref, w2_ref, o_ref, xb_scr, acc_scr):
    b = pl.program_id(0)
    f = pl.program_id(1)
    active = b < na_ref[0]
    rows, d = xb_scr.shape

    @pl.when(jnp.logical_and(active, f == 0))
    def _():
        chunks = [x_ref[pl.ds(c, rows, stride=SUBLANES), :] for c in range(d // LANES)]
        xb_scr[...] = jnp.concatenate(chunks, axis=-1).astype(BF16)
        acc_scr[...] = jnp.zeros_like(acc_scr)

    @pl.when(active)
    def _():
        x = xb_scr[...]
        a = _dot(x, w1_ref[0])
        g = _dot(x, w3_ref[0])
        act = (a * jax.nn.sigmoid(a) * g).astype(BF16)
        acc_scr[...] += _dot(act, w2_ref[0])

    last = f == pl.num_programs(1) - 1

    @pl.when(jnp.logical_and(active, last))
    def _():
        for c in range(d // LANES):
            o_ref[pl.ds(c, rows, stride=SUBLANES), :] = acc_scr[:, c * LANES:(c + 1) * LANES]

    @pl.when(jnp.logical_and(jnp.logical_not(active), last))
    def _():
        o_ref[...] = jnp.zeros_like(o_ref)


def _experts(blk_expert, n_active, xr2, w1, w3, w2, tf):
    n8 = xr2.shape[0]
    rows = MOE_PAD
    _, d, ff = w1.shape
    return pl.pallas_call(
        _experts_kernel,
        grid_spec=pltpu.PrefetchScalarGridSpec(
            num_scalar_prefetch=2,
            grid=(n8 // (rows * SUBLANES), ff // tf),
            in_specs=[
                pl.BlockSpec((rows * SUBLANES, LANES), lambda b, f, be, na: (b, 0)),
                pl.BlockSpec((1, d, tf), lambda b, f, be, na: (be[b], 0, f)),
                pl.BlockSpec((1, d, tf), lambda b, f, be, na: (be[b], 0, f)),
                pl.BlockSpec((1, tf, d), lambda b, f, be, na: (be[b], f, 0)),
            ],
            out_specs=pl.BlockSpec((rows * SUBLANES, LANES), lambda b, f, be, na: (b, 0)),
            scratch_shapes=[pltpu.VMEM((rows, d), BF16), pltpu.VMEM((rows, d), F32)],
        ),
        out_shape=jax.ShapeDtypeStruct((n8, LANES), F32),
        compiler_params=_cparams(("parallel", "arbitrary"), 48),
        name="moe_experts",
    )(blk_expert, n_active, xr2, w1, w3, w2)


def _combine_kernel(d0_ref, d1_ref, x_ref, mod_ref, info_ref, fg_ref, y_hbm, o_ref, buf, sem):
    tc, d = x_ref.shape
    base = pl.program_id(0) * tc

    def copy(src_row, slot, t):
        src = pl.multiple_of(src_row * SUBLANES, SUBLANES)
        dst = pl.multiple_of(t * SUBLANES, SUBLANES)
        return pltpu.make_async_copy(y_hbm.at[pl.ds(src, SUBLANES), :],
                                     buf.at[slot, pl.ds(dst, SUBLANES), :], sem)

    def issue(t, carry):
        copy(d0_ref[base + t], 0, t).start()
        copy(d1_ref[base + t], 1, t).start()
        return carry

    lax.fori_loop(0, tc, issue, 0)

    def wait(t, carry):
        copy(0, 0, 0).wait()
        copy(0, 1, 0).wait()
        return carry

    lax.fori_loop(0, tc, wait, 0)

    g0 = info_ref[:, 4:5]
    g1 = info_ref[:, 5:6]
    chunks = []
    for c in range(d // LANES):
        cols = slice(c * LANES, (c + 1) * LANES)
        y = g0 * buf[0, pl.ds(c, tc, stride=SUBLANES), :] + g1 * buf[1, pl.ds(c, tc, stride=SUBLANES), :]
        chunks.append(x_ref[:, cols] + mod_ref[0, 5:6, cols] * y)
    x4 = jnp.concatenate(chunks, axis=-1)
    o_ref[...] = x4 * lax.rsqrt(jnp.mean(x4 * x4, axis=-1, keepdims=True) + RMS_EPS) * fg_ref[...]


def _combine(dest0, dest1, x3, mod, info, final_g, yr2, seq, tc):
    t, d = x3.shape
    return pl.pallas_call(
        _combine_kernel,
        grid_spec=pltpu.PrefetchScalarGridSpec(
            num_scalar_prefetch=2,
            grid=(t // tc,),
            in_specs=[
                pl.BlockSpec((tc, d), lambda i, a, b: (i, 0)),
                pl.BlockSpec((1, 6, d), _row_block(tc, seq)),
                pl.BlockSpec((tc, LANES), lambda i, a, b: (i, 0)),
                pl.BlockSpec((1, d), lambda i, a, b: (0, 0)),
                pl.BlockSpec(memory_space=pl.ANY),
            ],
            out_specs=pl.BlockSpec((tc, d), lambda i, a, b: (i, 0)),
            scratch_shapes=[pltpu.VMEM((TOP_K, tc * SUBLANES, LANES), F32), pltpu.SemaphoreType.DMA(())],
        ),
        out_shape=jax.ShapeDtypeStruct((t, d), F32),
        compiler_params=_cparams(("arbitrary",), 32),
        name="moe_combine_final",
    )(dest0, dest1, x3, mod, info, final_g.reshape(1, d), yr2)


def _moe(x3, mod, h8, info, cnt, w1, w3, w2, final_g, seq):
    t, d = x3.shape
    n_experts = w1.shape[0]
    e0 = info[:, 0].astype(jnp.int32)
    e1 = info[:, 1].astype(jnp.int32)
    r0 = info[:, 2].astype(jnp.int32)
    r1 = info[:, 3].astype(jnp.int32)
    counts = cnt[0, :n_experts].astype(jnp.int32)
    padded = ((counts + MOE_PAD - 1) // MOE_PAD) * MOE_PAD
    pad_end = jnp.cumsum(padded)
    pad_start = pad_end - padded
    dest0 = pad_start[e0] + r0
    dest1 = pad_start[e1] + r1
    n_rows = t * TOP_K + n_experts * MOE_PAD
    tok = jnp.arange(t, dtype=jnp.int32)
    row_tok = jnp.zeros((n_rows,), jnp.int32).at[jnp.concatenate([dest0, dest1])].set(
        jnp.concatenate([tok, tok]))
    n_blk = n_rows // MOE_PAD
    blk_start = jnp.arange(n_blk, dtype=jnp.int32) * MOE_PAD
    blk_expert = jnp.minimum(jnp.searchsorted(pad_end, blk_start, side='right'),
                             n_experts - 1).astype(jnp.int32)
    n_active = (pad_end[-1:] // MOE_PAD).astype(jnp.int32)

    xr = _gather_rows(row_tok, h8.reshape(t, SUBLANES, LANES), MOE_PAD)
    yr2 = _experts(blk_expert, n_active, xr.reshape(n_rows * SUBLANES, LANES),
                   w1.astype(BF16), w3.astype(BF16), w2.astype(BF16), w1.shape[2] // 2)
    return _combine(dest0, dest1, x3, mod, info, final_g, yr2, seq, min(256, seq))


def kernel(x, c, ada_w, ada_b, norm_mix_g, norm_ffn_g, final_g, ab_w_in, sgu_v_g, sgu_w_s, sgu_b_s, s5_lam_re, s5_lam_im, s5_log_dt, s5_b_re, s5_b_im, s5_c_re, s5_c_im, s5_d, s5_w_glu, ab_w_out, ffn_w1, ffn_w3, ffn_w2, mla_w_in, mla_q_norm_g, mla_kv_norm_g, mla_w_uq, mla_w_ukv, mla_w_o, moe_w_router, moe_w1, moe_w3, moe_w2):
    bn, seq, d = x.shape
    t = bn * seq
    tm = min(512, seq)
    x2 = x.reshape(t, d)
    mod = _adaln(c, ada_w, ada_b)

    a_out, u_tiles = _l0_in(x2, mod[0], norm_mix_g[0], ab_w_in[0], sgu_w_s[0], sgu_b_s[0], sgu_v_g[0], seq, tm)
    t_mat, ws, wy, aq = _s5_weights(s5_lam_re[0], s5_lam_im[0], s5_log_dt[0], s5_b_re[0], s5_b_im[0],
                                    s5_c_re[0], s5_c_im[0], S5_Q)
    y5 = _s5(u_tiles, t_mat, ws, wy, aq, bn, S5_Q)
    x2 = _l0_out(x2, mod[0], a_out, y5, u_tiles, s5_d[0], s5_w_glu[0], ab_w_out[0], seq, tm)
    x2 = _ffn(x2, mod[0], norm_ffn_g[0], ffn_w1[0], ffn_w3[0], ffn_w2[0], seq, tm, ffn_w1.shape[2] // 2)

    q, k, v = _mla_proj(x2, mod[1], norm_mix_g[1], mla_w_in[0], mla_q_norm_g[0], mla_kv_norm_g[0],
                        mla_w_uq[0], mla_w_ukv[0], seq, tm)
    o = _attention(q, k, v, bn, seq, min(512, seq), min(512, seq))
    x3, h8, info, cnt = _attn_out(x2, mod[1], o, mla_w_o[0], norm_ffn_g[1], moe_w_router[0], seq, tm)
    out = _moe(x3, mod[1], h8, info, cnt, moe_w1[0], moe_w3[0], moe_w2[0], final_g, seq)
    return out.reshape(bn, seq, d)
```

```python
import functools
import math

import jax
import jax.numpy as jnp
from jax import lax
from jax.experimental import pallas as pl
from jax.experimental.pallas import tpu as pltpu

F32 = jnp.float32
BF16 = jnp.bfloat16
HIGHEST = lax.Precision.HIGHEST

LANES = 128
SUBLANES = 8
MIB = 1024 * 1024

RMS_EPS = 1e-6
C_HEADS = 8
QK_NOPE = 128
QK_ROPE = 64
V_HEAD = 128
ROPE_THETA = 10000.0
TOP_K = 2

S5_Q = 8
MOE_PAD = 512


def _cparams(semantics, vmem_mib):
    return pltpu.CompilerParams(dimension_semantics=semantics, vmem_limit_bytes=vmem_mib * MIB)


def _dot(a, b):
    return jnp.dot(a, b, preferred_element_type=F32)


def _norm_mod(x, g, shift, scale):
    y = x * lax.rsqrt(jnp.mean(x * x, axis=-1, keepdims=True) + RMS_EPS)
    return (y * g) * (1.0 + scale) + shift


def _row_block(tm, seq):
    per_batch = seq // tm
    return lambda i, *_: (i // per_batch, 0, 0)


def _adaln_kernel(c_ref, w_ref, b_ref, o_ref):
    c = c_ref[...]
    ca = c * jax.nn.sigmoid(c)
    o_ref[0] = jnp.dot(ca, w_ref[0], preferred_element_type=F32, precision=HIGHEST) + b_ref[0]


def _adaln(c, ada_w, ada_b):
    depth, d, n = ada_w.shape
    bn = c.shape[0]
    tn = n // 4
    c8 = jnp.zeros((SUBLANES, d), F32).at[:bn].set(c)
    out = pl.pallas_call(
        _adaln_kernel,
        grid=(depth, n // tn),
        in_specs=[
            pl.BlockSpec((SUBLANES, d), lambda l, j: (0, 0)),
            pl.BlockSpec((1, d, tn), lambda l, j: (l, 0, j)),
            pl.BlockSpec((1, 1, tn), lambda l, j: (l, 0, j)),
        ],
        out_specs=pl.BlockSpec((1, SUBLANES, tn), lambda l, j: (l, 0, j)),
        out_shape=jax.ShapeDtypeStruct((depth, SUBLANES, n), F32),
        compiler_params=_cparams(("parallel", "parallel"), 32),
        name="adaln",
    )(c8, ada_w, ada_b.reshape(depth, 1, n))
    return out[:, :bn].reshape(depth, bn, 6, d)


def _l0_in_kernel(x_ref, mod_ref, g_ref, win_ref, ws_ref, bs_ref, vg_ref, a_ref, u_ref, *,
                  heads, chunk, n_tiles):
    x = x_ref[...]
    h = _norm_mod(x, g_ref[...], mod_ref[0, 0:1, :], mod_ref[0, 1:2, :]).astype(BF16)
    z = _dot(h, win_ref[...])
    tm = x.shape[0]
    a_width = heads * LANES
    for g in range(heads):
        u = jax.nn.gelu(z[:, g * LANES:(g + 1) * LANES])
        v = jax.nn.gelu(z[:, a_width + g * LANES:a_width + (g + 1) * LANES])
        vn = v * lax.rsqrt(jnp.mean(v * v, axis=-1, keepdims=True) + RMS_EPS) * vg_ref[g:g + 1, :]
        vnb = vn.astype(BF16)
        bias = bs_ref[:, g:g + 1]
        for ci in range(tm // chunk):
            rows = slice(ci * chunk, (ci + 1) * chunk)
            s = _dot(ws_ref[g], vnb[rows]) + bias
            a_ref[rows, g * LANES:(g + 1) * LANES] = (u[rows] * s).astype(BF16)
    for j in range(n_tiles):
        u_ref[j] = z[:, 2 * a_width + j * LANES:2 * a_width + (j + 1) * LANES].astype(BF16)


def _l0_in(x2, mod, norm_g, w_in, w_s, b_s, v_g, seq, tm):
    t, d = x2.shape
    heads, chunk, _ = w_s.shape
    a_width = heads * LANES
    b_width = w_in.shape[1] - 2 * a_width
    n_tiles = b_width // LANES
    kern = functools.partial(_l0_in_kernel, heads=heads, chunk=chunk, n_tiles=n_tiles)
    return pl.pallas_call(
        kern,
        grid=(t // tm,),
        in_specs=[
            pl.BlockSpec((tm, d), lambda i: (i, 0)),
            pl.BlockSpec((1, 6, d), _row_block(tm, seq)),
            pl.BlockSpec((1, d), lambda i: (0, 0)),
            pl.BlockSpec(w_in.shape, lambda i: (0, 0)),
            pl.BlockSpec(w_s.shape, lambda i: (0, 0, 0)),
            pl.BlockSpec((chunk, heads), lambda i: (0, 0)),
            pl.BlockSpec((heads, LANES), lambda i: (0, 0)),
        ],
        out_specs=[
            pl.BlockSpec((tm, a_width), lambda i: (i, 0)),
            pl.BlockSpec((n_tiles, tm, LANES), lambda i: (0, i, 0)),
        ],
        out_shape=[
            jax.ShapeDtypeStruct((t, a_width), BF16),
            jax.ShapeDtypeStruct((n_tiles, t, LANES), BF16),
        ],
        compiler_params=_cparams(("parallel",), 40),
        name="l0_in_sgu",
    )(x2, mod, norm_g.reshape(1, d), w_in.astype(BF16), w_s.astype(BF16), b_s.T, v_g)


def _s5_weights(lam_re, lam_im, log_dt, b_re, b_im, c_re, c_im, q):
    _, groups, p = lam_re.shape
    h = b_re.shape[-1]
    gpt = LANES // h
    nt = groups // gpt
    dt = jnp.exp(log_dt.astype(F32))[..., None]
    mag = jnp.exp(lam_re * dt)
    ang = lam_im * dt
    ar, ai = mag * jnp.cos(ang), mag * jnp.sin(ang)
    den = lam_re * lam_re + lam_im * lam_im
    fr = ((ar - 1.0) * lam_re + ai * lam_im) / den
    fi = (ai * lam_re - (ar - 1.0) * lam_im) / den
    bbr = fr[..., None] * b_re - fi[..., None] * b_im
    bbi = fr[..., None] * b_im + fi[..., None] * b_re
    prs, pis = [jnp.ones_like(ar)], [jnp.zeros_like(ar)]
    for _ in range(q):
        prs.append(prs[-1] * ar - pis[-1] * ai)
        pis.append(prs[-2] * ai + pis[-1] * ar)
    pr, pi = jnp.stack(prs), jnp.stack(pis)
    abr = pr[..., None] * bbr - pi[..., None] * bbi
    abi = pr[..., None] * bbi + pi[..., None] * bbr
    car = c_re * pr[:, :, :, None, :] - c_im * pi[:, :, :, None, :]
    cai = c_re * pi[:, :, :, None, :] + c_im * pr[:, :, :, None, :]
    kk = (jnp.einsum('dgop,kdgpi->kdgoi', c_re, abr[:q], precision=HIGHEST)
          - jnp.einsum('dgop,kdgpi->kdgoi', c_im, abi[:q], precision=HIGHEST))
    eye = jnp.eye(gpt, dtype=F32)
    idx = jnp.arange(q)
    dfi = idx[None, :] - idx[:, None]
    kf = jnp.where((dfi >= 0)[:, :, None, None, None], kk[:, 0][jnp.clip(dfi, 0, q - 1)], 0.0)
    kb = jnp.where((dfi <= 0)[:, :, None, None, None], kk[:, 1][jnp.clip(-dfi, 0, q - 1)], 0.0)
    m = (kf + kb).reshape(q, q, nt, gpt, h, h)
    t_mat = jnp.einsum('sijgoa,gh->jsgaiho', m, eye).reshape(nt, q * LANES, q * LANES)

    w4 = jnp.stack([jnp.stack([abr[q - 1 - idx, 0], abi[q - 1 - idx, 0]]),
                    jnp.stack([abr[idx, 1], abi[idx, 1]])])
    w4 = w4.reshape(2, 2, q, nt, gpt, p, h)
    ws = jnp.einsum('drsjgpa,gh->jsgadrhp', w4, eye).reshape(nt, q * LANES, 4 * gpt * p)

    y4 = jnp.stack([jnp.stack([car[idx + 1, 0], -cai[idx + 1, 0]]),
                    jnp.stack([car[q - idx, 1], -cai[q - idx, 1]])])
    y4 = y4.reshape(2, 2, q, nt, gpt, h, p)
    wy = jnp.einsum('drijgop,gh->jdrgpiho', y4, eye).reshape(nt, 4 * gpt * p, q * LANES)

    aq = jnp.stack([jnp.stack([pr[q, 0], pi[q, 0]]), jnp.stack([pr[q, 1], pi[q, 1]])])
    aq = aq.reshape(2, 2, nt, gpt, p).transpose(2, 0, 1, 3, 4).reshape(nt, 1, 4 * gpt * p)
    return t_mat.astype(BF16), ws.astype(BF16), wy.astype(BF16), aq


def _s5_kernel(u_ref, t_ref, ws_ref, wy_ref, aq_ref, y_ref, s_ref):
    u = u_ref[0]
    s_ref[...] = _dot(u, ws_ref[0])
    rows, width = s_ref.shape
    hw = width // 4
    aq = aq_ref[0]
    afr, afi = aq[:, 0:hw], aq[:, hw:2 * hw]
    abr, abi = aq[:, 2 * hw:3 * hw], aq[:, 3 * hw:4 * hw]

    def body(c, carry):
        xfr, xfi, xbr, xbi = carry
        rf = pl.ds(c, 1)
        rb = pl.ds(rows - 1 - c, 1)
        sfr = s_ref[rf, 0:hw]
        sfi = s_ref[rf, hw:2 * hw]
        sbr = s_ref[rb, 2 * hw:3 * hw]
        sbi = s_ref[rb, 3 * hw:4 * hw]
        s_ref[rf, 0:hw] = xfr
        s_ref[rf, hw:2 * hw] = xfi
        s_ref[rb, 2 * hw:3 * hw] = xbr
        s_ref[rb, 3 * hw:4 * hw] = xbi
        return (afr * xfr - afi * xfi + sfr, afr * xfi + afi * xfr + sfi,
                abr * xbr - abi * xbi + sbr, abr * xbi + abi * xbr + sbi)

    zero = jnp.zeros((1, hw), F32)
    lax.fori_loop(0, rows, body, (zero, zero, zero, zero), unroll=8)
    y_ref[0] = _dot(u, t_ref[0]) + _dot(s_ref[...].astype(BF16), wy_ref[0])


def _s5(u_tiles, t_mat, ws, wy, aq, bn, q):
    nt, t, _ = u_tiles.shape
    rows = t // q // bn
    u2 = u_tiles.reshape(nt, t // q, q * LANES)
    sw = ws.shape[2]
    y2 = pl.pallas_call(
        _s5_kernel,
        grid=(nt, bn),
        in_specs=[
            pl.BlockSpec((1, rows, q * LANES), lambda j, b: (j, b, 0)),
            pl.BlockSpec((1,) + t_mat.shape[1:], lambda j, b: (j, 0, 0)),
            pl.BlockSpec((1,) + ws.shape[1:], lambda j, b: (j, 0, 0)),
            pl.BlockSpec((1,) + wy.shape[1:], lambda j, b: (j, 0, 0)),
            pl.BlockSpec((1, 1, sw), lambda j, b: (j, 0, 0)),
        ],
        out_specs=pl.BlockSpec((1, rows, q * LANES), lambda j, b: (j, b, 0)),
        out_shape=jax.ShapeDtypeStruct((nt, t // q, q * LANES), F32),
        scratch_shapes=[pltpu.VMEM((rows, sw), F32)],
        compiler_params=_cparams(("parallel", "parallel"), 48),
        name="s5_chunked",
    )(u2, t_mat, ws, wy, aq)
    return y2.reshape(nt, t, LANES)


def _l0_out_kernel(x_ref, mod_ref, a_ref, y5_ref, u_ref, d_ref, wglu_ref, wout_ref, o_ref, *, n_tiles):
    ys = []
    for j in range(n_tiles):
        ys.append(jax.nn.gelu(y5_ref[j] + d_ref[j] * u_ref[j].astype(F32)))
    y = jnp.concatenate(ys, axis=-1)
    glu = jax.nn.sigmoid(_dot(y.astype(BF16), wglu_ref[...]))
    b_out = (y * glu).astype(BF16)
    a_width = a_ref.shape[1]
    out = _dot(a_ref[...], wout_ref[0:a_width, :]) + _dot(b_out, wout_ref[a_width:, :])
    o_ref[...] = x_ref[...] + mod_ref[0, 2:3, :] * out


def _l0_out(x2, mod, a_out, y5, u_tiles, d_skip, w_glu, w_out, seq, tm):
    t, d = x2.shape
    n_tiles = u_tiles.shape[0]
    a_width = a_out.shape[1]
    kern = functools.partial(_l0_out_kernel, n_tiles=n_tiles)
    return pl.pallas_call(
        kern,
        grid=(t // tm,),
        in_specs=[
            pl.BlockSpec((tm, d), lambda i: (i, 0)),
            pl.BlockSpec((1, 6, d), _row_block(tm, seq)),
            pl.BlockSpec((tm, a_width), lambda i: (i, 0)),
            pl.BlockSpec((n_tiles, tm, LANES), lambda i: (0, i, 0)),
            pl.BlockSpec((n_tiles, tm, LANES), lambda i: (0, i, 0)),
            pl.BlockSpec((n_tiles, 1, LANES), lambda i: (0, 0, 0)),
            pl.BlockSpec(w_glu.shape, lambda i: (0, 0)),
            pl.BlockSpec(w_out.shape, lambda i: (0, 0)),
        ],
        out_specs=pl.BlockSpec((tm, d), lambda i: (i, 0)),
        out_shape=jax.ShapeDtypeStruct((t, d), F32),
        compiler_params=_cparams(("parallel",), 40),
        name="l0_out",
    )(x2, mod, a_out, y5, u_tiles, d_skip.reshape(n_tiles, 1, LANES),
      w_glu.astype(BF16), w_out.astype(BF16))


def _ffn_kernel(x_ref, mod_ref, g_ref, w1_ref, w3_ref, w2_ref, o_ref, h_scr, acc_scr):
    f = pl.program_id(1)

    @pl.when(f == 0)
    def _():
        h_scr[...] = _norm_mod(x_ref[...], g_ref[...], mod_ref[0, 3:4, :], mod_ref[0, 4:5, :]).astype(BF16)
        acc_scr[...] = jnp.zeros_like(acc_scr)

    h = h_scr[...]
    a = _dot(h, w1_ref[...])
    b = _dot(h, w3_ref[...])
    act = (a * jax.nn.sigmoid(a) * b).astype(BF16)
    acc_scr[...] += _dot(act, w2_ref[...])

    @pl.when(f == pl.num_programs(1) - 1)
    def _():
        o_ref[...] = x_ref[...] + mod_ref[0, 5:6, :] * acc_scr[...]


def _ffn(x2, mod, norm_g, w1, w3, w2, seq, tm, tf):
    t, d = x2.shape
    ff = w1.shape[1]
    return pl.pallas_call(
        _ffn_kernel,
        grid=(t // tm, ff // tf),
        in_specs=[
            pl.BlockSpec((tm, d), lambda i, f: (i, 0)),
            pl.BlockSpec((1, 6, d), _row_block(tm, seq)),
            pl.BlockSpec((1, d), lambda i, f: (0, 0)),
            pl.BlockSpec((d, tf), lambda i, f: (0, f)),
            pl.BlockSpec((d, tf), lambda i, f: (0, f)),
            pl.BlockSpec((tf, d), lambda i, f: (f, 0)),
        ],
        out_specs=pl.BlockSpec((tm, d), lambda i, f: (i, 0)),
        out_shape=jax.ShapeDtypeStruct((t, d), F32),
        scratch_shapes=[pltpu.VMEM((tm, d), BF16), pltpu.VMEM((tm, d), F32)],
        compiler_params=_cparams(("parallel", "arbitrary"), 48),
        name="ffn_swiglu",
    )(x2, mod, norm_g.reshape(1, d), w1.astype(BF16), w3.astype(BF16), w2.astype(BF16))


_NT_DIMS = (((1,), (1,)), ((), ()))
_TN_DIMS = (((0,), (0,)), ((), ()))


def _mla_proj_kernel(x_ref, mod_ref, g_ref, win_ref, gq_ref, gkv_ref, wqt_ref, wkn_ref, wvt_ref,
                     cos_ref, sin_ref, cost_ref, sint_ref, qt_ref, k_ref, vt_ref, *, q_lora, kv_lora, heads):
    h = _norm_mod(x_ref[...], g_ref[...], mod_ref[0, 0:1, :], mod_ref[0, 1:2, :]).astype(BF16)
    z = _dot(h, win_ref[...])
    cq = z[:, :q_lora]
    cq = (cq * lax.rsqrt(jnp.mean(cq * cq, axis=-1, keepdims=True) + RMS_EPS) * gq_ref[...]).astype(BF16)
    ckv = z[:, q_lora:q_lora + kv_lora]
    ckv = (ckv * lax.rsqrt(jnp.mean(ckv * ckv, axis=-1, keepdims=True) + RMS_EPS) * gkv_ref[...]).astype(BF16)
    r0 = q_lora + kv_lora
    k_rope = (z[:, r0:r0 + LANES] * cos_ref[...] + z[:, r0 + LANES:r0 + 2 * LANES] * sin_ref[...]).astype(BF16)
    kn = _dot(ckv, wkn_ref[...])
    qat = lax.dot_general(wqt_ref[...], cq, _NT_DIMS, preferred_element_type=F32)
    vt_ref[0] = lax.dot_general(wvt_ref[...], ckv, _NT_DIMS, preferred_element_type=F32).astype(BF16)
    cost = cost_ref[...]
    sint = sint_ref[...]
    hw = heads * LANES
    for hd in range(heads):
        c = slice(hd * LANES, (hd + 1) * LANES)
        qt_ref[2 * hd * LANES:(2 * hd + 1) * LANES, :] = qat[c, :].astype(BF16)
        q_rope = qat[hw + hd * LANES:hw + (hd + 1) * LANES, :] * cost \
            + qat[2 * hw + hd * LANES:2 * hw + (hd + 1) * LANES, :] * sint
        qt_ref[(2 * hd + 1) * LANES:(2 * hd + 2) * LANES, :] = q_rope.astype(BF16)
        k_ref[:, 2 * hd * LANES:(2 * hd + 1) * LANES] = kn[:, c].astype(BF16)
        k_ref[:, (2 * hd + 1) * LANES:(2 * hd + 2) * LANES] = k_rope


def _rope_pad(w_rope):
    half = QK_ROPE // 2
    x1, x2 = w_rope[..., :half], w_rope[..., half:]
    zeros = jnp.zeros(w_rope.shape[:-1] + (LANES - QK_ROPE,), w_rope.dtype)
    return jnp.concatenate([x1, x2, zeros], -1), jnp.concatenate([-x2, x1, zeros], -1)


def _mla_proj(x2, mod, norm_g, w_in, gq, gkv, w_uq, w_ukv, seq, tm):
    t, d = x2.shape
    q_lora, kv_lora = gq.shape[0], gkv.shape[0]
    heads = C_HEADS
    scale = math.log2(math.e) / math.sqrt(QK_NOPE + QK_ROPE)
    kr, krs = _rope_pad(w_in[:, q_lora + kv_lora:])
    w_in_ext = jnp.concatenate([w_in[:, :q_lora + kv_lora], kr, krs], -1).astype(BF16)
    wq3 = (w_uq * scale).reshape(q_lora, heads, QK_NOPE + QK_ROPE)
    qr, qrs = _rope_pad(wq3[..., QK_NOPE:])
    wqt = jnp.concatenate([wq3[..., :QK_NOPE].reshape(q_lora, -1), qr.reshape(q_lora, -1),
                           qrs.reshape(q_lora, -1)], -1).T.astype(BF16)
    wkv3 = w_ukv.reshape(kv_lora, heads, QK_NOPE + V_HEAD)
    wkn = wkv3[..., :QK_NOPE].reshape(kv_lora, -1).astype(BF16)
    wvt = wkv3[..., QK_NOPE:].reshape(kv_lora, -1).T.astype(BF16)
    inv_freq = ROPE_THETA ** (-jnp.arange(0, QK_ROPE, 2, dtype=F32) / QK_ROPE)
    ang = jnp.arange(seq, dtype=F32)[:, None] * inv_freq[None, :]
    pad = jnp.zeros((seq, LANES - QK_ROPE), F32)
    cos = jnp.concatenate([jnp.cos(ang), jnp.cos(ang), pad], -1)
    sin = jnp.concatenate([jnp.sin(ang), jnp.sin(ang), pad], -1)
    per_batch = seq // tm
    kern = functools.partial(_mla_proj_kernel, q_lora=q_lora, kv_lora=kv_lora, heads=heads)
    const = lambda i: (0, 0)
    return pl.pallas_call(
        kern,
        grid=(t // tm,),
        in_specs=[
            pl.BlockSpec((tm, d), lambda i: (i, 0)),
            pl.BlockSpec((1, 6, d), _row_block(tm, seq)),
            pl.BlockSpec((1, d), const),
            pl.BlockSpec(w_in_ext.shape, const),
            pl.BlockSpec((1, q_lora), const),
            pl.BlockSpec((1, kv_lora), const),
            pl.BlockSpec(wqt.shape, const),
            pl.BlockSpec(wkn.shape, const),
            pl.BlockSpec(wvt.shape, const),
            pl.BlockSpec((tm, LANES), lambda i: (i % per_batch, 0)),
            pl.BlockSpec((tm, LANES), lambda i: (i % per_batch, 0)),
            pl.BlockSpec((LANES, tm), lambda i: (0, i % per_batch)),
            pl.BlockSpec((LANES, tm), lambda i: (0, i % per_batch)),
        ],
        out_specs=[
            pl.BlockSpec((2 * heads * LANES, tm), lambda i: (0, i)),
            pl.BlockSpec((tm, 2 * heads * LANES), lambda i: (i, 0)),
            pl.BlockSpec((1, heads * V_HEAD, tm), lambda i: (i, 0, 0)),
        ],
        out_shape=[
            jax.ShapeDtypeStruct((2 * heads * LANES, t), BF16),
            jax.ShapeDtypeStruct((t, 2 * heads * LANES), BF16),
            jax.ShapeDtypeStruct((t // tm, heads * V_HEAD, tm), BF16),
        ],
        compiler_params=_cparams(("parallel",), 48),
        name="mla_proj",
    )(x2, mod, norm_g.reshape(1, d), w_in_ext, gq.reshape(1, -1), gkv.reshape(1, -1), wqt, wkn, wvt,
      cos, sin, cos.T, sin.T)


def _attn_kernel(qt_ref, k_ref, vt_ref, o_ref, s_scr, acc_scr, *, tk):
    tq = qt_ref.shape[1]
    n_kv = k_ref.shape[0] // tk
    tv = vt_ref.shape[2]
    sub = tk // tv

    def scores(c, slot):
        k = k_ref[pl.ds(pl.multiple_of(c * tk, tk), tk), :]
        s_scr[slot] = _dot(k, qt_ref[...])

    def update(c, slot, m, l):
        s = s_scr[slot]
        m_new = jnp.maximum(m, jnp.max(s, axis=0, keepdims=True))
        alpha = jnp.exp2(m - m_new)
        p = jnp.exp2(s - m_new)
        l = alpha * l + jnp.sum(p, axis=0, keepdims=True)
        pb = p.astype(BF16)
        acc = alpha * acc_scr[...]
        for j in range(sub):
            acc = acc + _dot(vt_ref[c * sub + j], pb[j * tv:(j + 1) * tv])
        acc_scr[...] = acc
        return m_new, l

    def body(c2, carry):
        m, l = carry
        c = 2 * c2
        scores(c + 1, 1)
        m, l = update(c, 0, m, l)
        scores(c + 2, 0)
        return update(c + 1, 1, m, l)

    acc_scr[...] = jnp.zeros_like(acc_scr)
    scores(0, 0)
    m, l = lax.fori_loop(0, n_kv // 2 - 1, body,
                         (jnp.full((1, tq), -jnp.inf, F32), jnp.zeros((1, tq), F32)))
    scores(n_kv - 1, 1)
    m, l = update(n_kv - 2, 0, m, l)
    m, l = update(n_kv - 1, 1, m, l)
    o_ref[0] = (acc_scr[...] / l).astype(BF16)


def _attention(qt, k, vt, bn, seq, tq, tk):
    tv = vt.shape[2]
    t = k.shape[0]
    heads = C_HEADS
    nq = seq // tq
    kern = functools.partial(_attn_kernel, tk=tk)
    return pl.pallas_call(
        kern,
        grid=(bn, heads, nq),
        in_specs=[
            pl.BlockSpec((2 * LANES, tq), lambda b, h, i: (h, b * nq + i)),
            pl.BlockSpec((seq, 2 * LANES), lambda b, h, i: (b, h)),
            pl.BlockSpec((seq // tv, V_HEAD, tv), lambda b, h, i: (b, h, 0)),
        ],
        out_specs=pl.BlockSpec((1, V_HEAD, tq), lambda b, h, i: (b * nq + i, h, 0)),
        out_shape=jax.ShapeDtypeStruct((t // tq, heads * V_HEAD, tq), BF16),
        scratch_shapes=[pltpu.VMEM((2, tk, tq), F32), pltpu.VMEM((V_HEAD, tq), F32)],
        compiler_params=_cparams(("parallel", "parallel", "parallel"), 48),
        name="mla_attention",
    )(qt, k, vt)


def _attn_out_kernel(x_ref, mod_ref, o_ref, wo_ref, g_ref, wr_ref, x3_ref, h_ref, info_ref, cnt_ref, run_scr,
                     *, n_experts):
    i = pl.program_id(0)

    @pl.when(i == 0)
    def _():
        run_scr[...] = jnp.zeros_like(run_scr)

    y = lax.dot_general(o_ref[0], wo_ref[...], _TN_DIMS, preferred_element_type=F32)
    x3 = x_ref[...] + mod_ref[0, 2:3, :] * y
    x3_ref[...] = x3
    h = _norm_mod(x3, g_ref[...], mod_ref[0, 3:4, :], mod_ref[0, 4:5, :])
    tm, d = h.shape
    for c in range(d // LANES):
        h_ref[pl.ds(c, tm, stride=SUBLANES), :] = h[:, c * LANES:(c + 1) * LANES]

    logits = jnp.dot(h, wr_ref[...], preferred_element_type=F32, precision=HIGHEST)
    lane = lax.broadcasted_iota(jnp.int32, (tm, LANES), 1).astype(F32)
    neg = jnp.float32(-jnp.inf)
    lg = jnp.where(lane < n_experts, logits, neg)
    m1 = jnp.max(lg, axis=-1, keepdims=True)
    i1 = jnp.min(jnp.where(lg == m1, lane, float(LANES)), axis=-1, keepdims=True)
    lg2 = jnp.where(lane == i1, neg, lg)
    m2 = jnp.max(lg2, axis=-1, keepdims=True)
    i2 = jnp.min(jnp.where(lg2 == m2, lane, float(LANES)), axis=-1, keepdims=True)
    e = jnp.exp(m2 - m1)
    g0 = 1.0 / (1.0 + e)
    g1 = e / (1.0 + e)
    sel1 = lane == i1
    sel2 = lane == i2
    onehot = jnp.where(sel1, 1.0, 0.0) + jnp.where(sel2, 1.0, 0.0)
    row = lax.broadcasted_iota(jnp.int32, (tm, tm), 0)
    col = lax.broadcasted_iota(jnp.int32, (tm, tm), 1)
    tri = jnp.where(col < row, 1.0, 0.0).astype(BF16)
    before = _dot(tri, onehot.astype(BF16)) + run_scr[...]
    r0 = jnp.sum(jnp.where(sel1, before, 0.0), axis=-1, keepdims=True)
    r1 = jnp.sum(jnp.where(sel2, before, 0.0), axis=-1, keepdims=True)
    run_scr[...] += jnp.sum(onehot, axis=0, keepdims=True)
    info = jnp.where(lane == 0, i1, jnp.where(lane == 1, i2, jnp.where(lane == 2, r0, jnp.where(
        lane == 3, r1, jnp.where(lane == 4, g0, jnp.where(lane == 5, g1, 0.0))))))
    info_ref[...] = info
    cnt_ref[...] = jnp.broadcast_to(run_scr[...], cnt_ref.shape)


def _attn_out(x2, mod, o, w_o, norm_g, w_router, seq, tm):
    t, d = x2.shape
    n_experts = w_router.shape[1]
    wr = jnp.zeros((d, LANES), F32).at[:, :n_experts].set(w_router)
    kern = functools.partial(_attn_out_kernel, n_experts=n_experts)
    const = lambda i: (0, 0)
    return pl.pallas_call(
        kern,
        grid=(t // tm,),
        in_specs=[
            pl.BlockSpec((tm, d), lambda i: (i, 0)),
            pl.BlockSpec((1, 6, d), _row_block(tm, seq)),
            pl.BlockSpec((1, o.shape[1], tm), lambda i: (i, 0, 0)),
            pl.BlockSpec(w_o.shape, const),
            pl.BlockSpec((1, d), const),
            pl.BlockSpec((d, LANES), const),
        ],
        out_specs=[
            pl.BlockSpec((tm, d), lambda i: (i, 0)),
            pl.BlockSpec((tm * SUBLANES, LANES), lambda i: (i, 0)),
            pl.BlockSpec((tm, LANES), lambda i: (i, 0)),
            pl.BlockSpec((SUBLANES, LANES), const),
        ],
        out_shape=[
            jax.ShapeDtypeStruct((t, d), F32),
            jax.ShapeDtypeStruct((t * SUBLANES, LANES), F32),
            jax.ShapeDtypeStruct((t, LANES), F32),
            jax.ShapeDtypeStruct((SUBLANES, LANES), F32),
        ],
        scratch_shapes=[pltpu.VMEM((1, LANES), F32)],
        compiler_params=_cparams(("arbitrary",), 40),
        name="attn_out_router",
    )(x2, mod, o, w_o.astype(BF16), norm_g.reshape(1, d), wr)


def _gather_kernel(tok_ref, h_hbm, o_ref, sem, *, rows):
    base = pl.program_id(0) * rows

    def copy(tok, r):
        return pltpu.make_async_copy(h_hbm.at[tok], o_ref.at[r], sem)

    def issue(r, carry):
        copy(tok_ref[base + r], r).start()
        return carry

    lax.fori_loop(0, rows, issue, 0, unroll=8)

    def wait(r, carry):
        copy(0, 0).wait()
        return carry

    lax.fori_loop(0, rows, wait, 0, unroll=8)


def _gather_rows(row_tok, h3, rows):
    n_rows = row_tok.shape[0]
    kern = functools.partial(_gather_kernel, rows=rows)
    return pl.pallas_call(
        kern,
        grid_spec=pltpu.PrefetchScalarGridSpec(
            num_scalar_prefetch=1,
            grid=(n_rows // rows,),
            in_specs=[pl.BlockSpec(memory_space=pl.ANY)],
            out_specs=pl.BlockSpec((rows,) + h3.shape[1:], lambda i, tok: (i, 0, 0)),
            scratch_shapes=[pltpu.SemaphoreType.DMA(())],
        ),
        out_shape=jax.ShapeDtypeStruct((n_rows,) + h3.shape[1:], h3.dtype),
        compiler_params=_cparams(("arbitrary",), 16),
        name="moe_gather",
    )(row_tok, h3)


def _experts_kernel(be_ref, na_ref, x_ref, w1_ref, w3_ref, w2_ref, o_ref, xb_scr, acc_scr):
    b = pl.program_id(0)
    f = pl.program_id(1)
    active = b < na_ref[0]
    rows, d = xb_scr.shape

    @pl.when(jnp.logical_and(active, f == 0))
    def _():
        chunks = [x_ref[pl.ds(c, rows, stride=SUBLANES), :] for c in range(d // LANES)]
        xb_scr[...] = jnp.concatenate(chunks, axis=-1).astype(BF16)
        acc_scr[...] = jnp.zeros_like(acc_scr)

    @pl.when(active)
    def _():
        x = xb_scr[...]
        a = _dot(x, w1_ref[0])
        g = _dot(x, w3_ref[0])
        act = (a * jax.nn.sigmoid(a) * g).astype(BF16)
        acc_scr[...] += _dot(act, w2_ref[0])

    last = f == pl.num_programs(1) - 1

    @pl.when(jnp.logical_and(active, last))
    def _():
        for c in range(d // LANES):
            o_ref[pl.ds(c, rows, stride=SUBLANES), :] = acc_scr[:, c * LANES:(c + 1) * LANES]

    @pl.when(jnp.logical_and(jnp.logical_not(active), last))
    def _():
        o_ref[...] = jnp.zeros_like(o_ref)


def _experts(blk_expert, n_active, xr2, w1, w3, w2, tf):
    n8 = xr2.shape[0]
    rows = MOE_PAD
    _, d, ff = w1.shape
    return pl.pallas_call(
        _experts_kernel,
        grid_spec=pltpu.PrefetchScalarGridSpec(
            num_scalar_prefetch=2,
            grid=(n8 // (rows * SUBLANES), ff // tf),
            in_specs=[
                pl.BlockSpec((rows * SUBLANES, LANES), lambda b, f, be, na: (b, 0)),
                pl.BlockSpec((1, d, tf), lambda b, f, be, na: (be[b], 0, f)),
                pl.BlockSpec((1, d, tf), lambda b, f, be, na: (be[b], 0, f)),
                pl.BlockSpec((1, tf, d), lambda b, f, be, na: (be[b], f, 0)),
            ],
            out_specs=pl.BlockSpec((rows * SUBLANES, LANES), lambda b, f, be, na: (b, 0)),
            scratch_shapes=[pltpu.VMEM((rows, d), BF16), pltpu.VMEM((rows, d), F32)],
        ),
        out_shape=jax.ShapeDtypeStruct((n8, LANES), F32),
        compiler_params=_cparams(("parallel", "arbitrary"), 48),
        name="moe_experts",
    )(blk_expert, n_active, xr2, w1, w3, w2)


def _combine_kernel(d0_ref, d1_ref, x_ref, mod_ref, info_ref, fg_ref, y_hbm, o_ref, buf, sem):
    tc, d = x_ref.shape
    base = pl.program_id(0) * tc

    def copy(src_row, slot, t):
        src = pl.multiple_of(src_row * SUBLANES, SUBLANES)
        dst = pl.multiple_of(t * SUBLANES, SUBLANES)
        return pltpu.make_async_copy(y_hbm.at[pl.ds(src, SUBLANES), :],
                                     buf.at[slot, pl.ds(dst, SUBLANES), :], sem)

    def issue(t, carry):
        copy(d0_ref[base + t], 0, t).start()
        copy(d1_ref[base + t], 1, t).start()
        return carry

    lax.fori_loop(0, tc, issue, 0)

    def wait(t, carry):
        copy(0, 0, 0).wait()
        copy(0, 1, 0).wait()
        return carry

    lax.fori_loop(0, tc, wait, 0)

    g0 = info_ref[:, 4:5]
    g1 = info_ref[:, 5:6]
    chunks = []
    for c in range(d // LANES):
        cols = slice(c * LANES, (c + 1) * LANES)
        y = g0 * buf[0, pl.ds(c, tc, stride=SUBLANES), :] + g1 * buf[1, pl.ds(c, tc, stride=SUBLANES), :]
        chunks.append(x_ref[:, cols] + mod_ref[0, 5:6, cols] * y)
    x4 = jnp.concatenate(chunks, axis=-1)
    o_ref[...] = x4 * lax.rsqrt(jnp.mean(x4 * x4, axis=-1, keepdims=True) + RMS_EPS) * fg_ref[...]


def _combine(dest0, dest1, x3, mod, info, final_g, yr2, seq, tc):
    t, d = x3.shape
    return pl.pallas_call(
        _combine_kernel,
        grid_spec=pltpu.PrefetchScalarGridSpec(
            num_scalar_prefetch=2,
            grid=(t // tc,),
            in_specs=[
                pl.BlockSpec((tc, d), lambda i, a, b: (i, 0)),
                pl.BlockSpec((1, 6, d), _row_block(tc, seq)),
                pl.BlockSpec((tc, LANES), lambda i, a, b: (i, 0)),
                pl.BlockSpec((1, d), lambda i, a, b: (0, 0)),
                pl.BlockSpec(memory_space=pl.ANY),
            ],
            out_specs=pl.BlockSpec((tc, d), lambda i, a, b: (i, 0)),
            scratch_shapes=[pltpu.VMEM((TOP_K, tc * SUBLANES, LANES), F32), pltpu.SemaphoreType.DMA(())],
        ),
        out_shape=jax.ShapeDtypeStruct((t, d), F32),
        compiler_params=_cparams(("arbitrary",), 32),
        name="moe_combine_final",
    )(dest0, dest1, x3, mod, info, final_g.reshape(1, d), yr2)


def _moe(x3, mod, h8, info, cnt, w1, w3, w2, final_g, seq):
    t, d = x3.shape
    n_experts = w1.shape[0]
    e0 = info[:, 0].astype(jnp.int32)
    e1 = info[:, 1].astype(jnp.int32)
    r0 = info[:, 2].astype(jnp.int32)
    r1 = info[:, 3].astype(jnp.int32)
    counts = cnt[0, :n_experts].astype(jnp.int32)
    padded = ((counts + MOE_PAD - 1) // MOE_PAD) * MOE_PAD
    pad_end = jnp.cumsum(padded)
    pad_start = pad_end - padded
    dest0 = pad_start[e0] + r0
    dest1 = pad_start[e1] + r1
    n_rows = t * TOP_K + n_experts * MOE_PAD
    tok = jnp.arange(t, dtype=jnp.int32)
    row_tok = jnp.zeros((n_rows,), jnp.int32).at[jnp.concatenate([dest0, dest1])].set(
        jnp.concatenate([tok, tok]))
    n_blk = n_rows // MOE_PAD
    blk_start = jnp.arange(n_blk, dtype=jnp.int32) * MOE_PAD
    blk_expert = jnp.minimum(jnp.searchsorted(pad_end, blk_start, side='right'),
                             n_experts - 1).astype(jnp.int32)
    n_active = (pad_end[-1:] // MOE_PAD).astype(jnp.int32)

    xr = _gather_rows(row_tok, h8.reshape(t, SUBLANES, LANES), MOE_PAD)
    yr2 = _experts(blk_expert, n_active, xr.reshape(n_rows * SUBLANES, LANES),
                   w1.astype(BF16), w3.astype(BF16), w2.astype(BF16), w1.shape[2] // 2)
    return _combine(dest0, dest1, x3, mod, info, final_g, yr2, seq, min(256, seq))


def kernel(x, c, ada_w, ada_b, norm_mix_g, norm_ffn_g, final_g, ab_w_in, sgu_v_g, sgu_w_s, sgu_b_s, s5_lam_re, s5_lam_im, s5_log_dt, s5_b_re, s5_b_im, s5_c_re, s5_c_im, s5_d, s5_w_glu, ab_w_out, ffn_w1, ffn_w3, ffn_w2, mla_w_in, mla_q_norm_g, mla_kv_norm_g, mla_w_uq, mla_w_ukv, mla_w_o, moe_w_router, moe_w1, moe_w3, moe_w2):
    bn, seq, d = x.shape
    t = bn * seq
    tm = min(512, seq)
    x2 = x.reshape(t, d)
    mod = _adaln(c, ada_w, ada_b)

    a_out, u_tiles = _l0_in(x2, mod[0], norm_mix_g[0], ab_w_in[0], sgu_w_s[0], sgu_b_s[0], sgu_v_g[0], seq, tm)
    t_mat, ws, wy, aq = _s5_weights(s5_lam_re[0], s5_lam_im[0], s5_log_dt[0], s5_b_re[0], s5_b_im[0],
                                    s5_c_re[0], s5_c_im[0], S5_Q)
    y5 = _s5(u_tiles, t_mat, ws, wy, aq, bn, S5_Q)
    x2 = _l0_out(x2, mod[0], a_out, y5, u_tiles, s5_d[0], s5_w_glu[0], ab_w_out[0], seq, tm)
    x2 = _ffn(x2, mod[0], norm_ffn_g[0], ffn_w1[0], ffn_w3[0], ffn_w2[0], seq, tm, ffn_w1.shape[2] // 2)

    qt, k, vt = _mla_proj(x2, mod[1], norm_mix_g[1], mla_w_in[0], mla_q_norm_g[0], mla_kv_norm_g[0],
                          mla_w_uq[0], mla_w_ukv[0], seq, tm)
    o = _attention(qt, k, vt, bn, seq, tm, min(1024, seq // 2))
    x3, h8, info, cnt = _attn_out(x2, mod[1], o, mla_w_o[0], norm_ffn_g[1], moe_w_router[0], seq, tm)
    out = _moe(x3, mod[1], h8, info, cnt, moe_w1[0], moe_w3[0], moe_w2[0], final_g, seq)
    return out.reshape(bn, seq, d)
```

```python
import functools
import math

import jax
import jax.numpy as jnp
from jax import lax
from jax.experimental import pallas as pl
from jax.experimental.pallas import tpu as pltpu

F32 = jnp.float32
BF16 = jnp.bfloat16
HIGHEST = lax.Precision.HIGHEST

LANES = 128
SUBLANES = 8
MIB = 1024 * 1024

RMS_EPS = 1e-6
C_HEADS = 8
QK_NOPE = 128
QK_ROPE = 64
V_HEAD = 128
ROPE_THETA = 10000.0
TOP_K = 2

S5_Q = 8
MOE_PAD = 512


def _cparams(semantics, vmem_mib):
    return pltpu.CompilerParams(dimension_semantics=semantics, vmem_limit_bytes=vmem_mib * MIB)


def _dot(a, b):
    return jnp.dot(a, b, preferred_element_type=F32)


def _norm_mod(x, g, shift, scale):
    y = x * lax.rsqrt(jnp.mean(x * x, axis=-1, keepdims=True) + RMS_EPS)
    return (y * g) * (1.0 + scale) + shift


def _row_block(tm, seq):
    per_batch = seq // tm
    return lambda i, *_: (i // per_batch, 0, 0)


def _adaln_kernel(c_ref, w_ref, b_ref, o_ref):
    c = c_ref[...]
    ca = c * jax.nn.sigmoid(c)
    o_ref[0] = jnp.dot(ca, w_ref[0], preferred_element_type=F32, precision=HIGHEST) + b_ref[0]


def _adaln(c, ada_w, ada_b):
    depth, d, n = ada_w.shape
    bn = c.shape[0]
    tn = n // 4
    c8 = jnp.zeros((SUBLANES, d), F32).at[:bn].set(c)
    out = pl.pallas_call(
        _adaln_kernel,
        grid=(depth, n // tn),
        in_specs=[
            pl.BlockSpec((SUBLANES, d), lambda l, j: (0, 0)),
            pl.BlockSpec((1, d, tn), lambda l, j: (l, 0, j)),
            pl.BlockSpec((1, 1, tn), lambda l, j: (l, 0, j)),
        ],
        out_specs=pl.BlockSpec((1, SUBLANES, tn), lambda l, j: (l, 0, j)),
        out_shape=jax.ShapeDtypeStruct((depth, SUBLANES, n), F32),
        compiler_params=_cparams(("parallel", "parallel"), 32),
        name="adaln",
    )(c8, ada_w, ada_b.reshape(depth, 1, n))
    return out[:, :bn].reshape(depth, bn, 6, d)


def _l0_in_kernel(x_ref, mod_ref, g_ref, win_ref, ws_ref, bs_ref, vg_ref, a_ref, u_ref, *,
                  heads, chunk, n_tiles):
    x = x_ref[...]
    h = _norm_mod(x, g_ref[...], mod_ref[0, 0:1, :], mod_ref[0, 1:2, :]).astype(BF16)
    z = _dot(h, win_ref[...])
    tm = x.shape[0]
    a_width = heads * LANES
    for g in range(heads):
        u = jax.nn.gelu(z[:, g * LANES:(g + 1) * LANES])
        v = jax.nn.gelu(z[:, a_width + g * LANES:a_width + (g + 1) * LANES])
        vn = v * lax.rsqrt(jnp.mean(v * v, axis=-1, keepdims=True) + RMS_EPS) * vg_ref[g:g + 1, :]
        vnb = vn.astype(BF16)
        bias = bs_ref[:, g:g + 1]
        for ci in range(tm // chunk):
            rows = slice(ci * chunk, (ci + 1) * chunk)
            s = _dot(ws_ref[g], vnb[rows]) + bias
            a_ref[rows, g * LANES:(g + 1) * LANES] = (u[rows] * s).astype(BF16)
    for j in range(n_tiles):
        u_ref[j] = z[:, 2 * a_width + j * LANES:2 * a_width + (j + 1) * LANES].astype(BF16)


def _l0_in(x2, mod, norm_g, w_in, w_s, b_s, v_g, seq, tm):
    t, d = x2.shape
    heads, chunk, _ = w_s.shape
    a_width = heads * LANES
    b_width = w_in.shape[1] - 2 * a_width
    n_tiles = b_width // LANES
    kern = functools.partial(_l0_in_kernel, heads=heads, chunk=chunk, n_tiles=n_tiles)
    return pl.pallas_call(
        kern,
        grid=(t // tm,),
        in_specs=[
            pl.BlockSpec((tm, d), lambda i: (i, 0)),
            pl.BlockSpec((1, 6, d), _row_block(tm, seq)),
            pl.BlockSpec((1, d), lambda i: (0, 0)),
            pl.BlockSpec(w_in.shape, lambda i: (0, 0)),
            pl.BlockSpec(w_s.shape, lambda i: (0, 0, 0)),
            pl.BlockSpec((chunk, heads), lambda i: (0, 0)),
            pl.BlockSpec((heads, LANES), lambda i: (0, 0)),
        ],
        out_specs=[
            pl.BlockSpec((tm, a_width), lambda i: (i, 0)),
            pl.BlockSpec((n_tiles, tm, LANES), lambda i: (0, i, 0)),
        ],
        out_shape=[
            jax.ShapeDtypeStruct((t, a_width), BF16),
            jax.ShapeDtypeStruct((n_tiles, t, LANES), BF16),
        ],
        compiler_params=_cparams(("parallel",), 40),
        name="l0_in_sgu",
    )(x2, mod, norm_g.reshape(1, d), w_in.astype(BF16), w_s.astype(BF16), b_s.T, v_g)


def _s5_weights(lam_re, lam_im, log_dt, b_re, b_im, c_re, c_im, q):
    _, groups, p = lam_re.shape
    h = b_re.shape[-1]
    gpt = LANES // h
    nt = groups // gpt
    dt = jnp.exp(log_dt.astype(F32))[..., None]
    mag = jnp.exp(lam_re * dt)
    ang = lam_im * dt
    ar, ai = mag * jnp.cos(ang), mag * jnp.sin(ang)
    den = lam_re * lam_re + lam_im * lam_im
    fr = ((ar - 1.0) * lam_re + ai * lam_im) / den
    fi = (ai * lam_re - (ar - 1.0) * lam_im) / den
    bbr = fr[..., None] * b_re - fi[..., None] * b_im
    bbi = fr[..., None] * b_im + fi[..., None] * b_re
    prs, pis = [jnp.ones_like(ar)], [jnp.zeros_like(ar)]
    for _ in range(q):
        prs.append(prs[-1] * ar - pis[-1] * ai)
        pis.append(prs[-2] * ai + pis[-1] * ar)
    pr, pi = jnp.stack(prs), jnp.stack(pis)
    abr = pr[..., None] * bbr - pi[..., None] * bbi
    abi = pr[..., None] * bbi + pi[..., None] * bbr
    car = c_re * pr[:, :, :, None, :] - c_im * pi[:, :, :, None, :]
    cai = c_re * pi[:, :, :, None, :] + c_im * pr[:, :, :, None, :]
    kk = (jnp.einsum('dgop,kdgpi->kdgoi', c_re, abr[:q], precision=HIGHEST)
          - jnp.einsum('dgop,kdgpi->kdgoi', c_im, abi[:q], precision=HIGHEST))
    eye = jnp.eye(gpt, dtype=F32)
    idx = jnp.arange(q)
    dfi = idx[None, :] - idx[:, None]
    kf = jnp.where((dfi >= 0)[:, :, None, None, None], kk[:, 0][jnp.clip(dfi, 0, q - 1)], 0.0)
    kb = jnp.where((dfi <= 0)[:, :, None, None, None], kk[:, 1][jnp.clip(-dfi, 0, q - 1)], 0.0)
    m = (kf + kb).reshape(q, q, nt, gpt, h, h)
    t_mat = jnp.einsum('sijgoa,gh->jsgaiho', m, eye).reshape(nt, q * LANES, q * LANES)

    w4 = jnp.stack([jnp.stack([abr[q - 1 - idx, 0], abi[q - 1 - idx, 0]]),
                    jnp.stack([abr[idx, 1], abi[idx, 1]])])
    w4 = w4.reshape(2, 2, q, nt, gpt, p, h)
    ws = jnp.einsum('drsjgpa,gh->jsgadrhp', w4, eye).reshape(nt, q * LANES, 4 * gpt * p)

    y4 = jnp.stack([jnp.stack([car[idx + 1, 0], -cai[idx + 1, 0]]),
                    jnp.stack([car[q - idx, 1], -cai[q - idx, 1]])])
    y4 = y4.reshape(2, 2, q, nt, gpt, h, p)
    wy = jnp.einsum('drijgop,gh->jdrgpiho', y4, eye).reshape(nt, 4 * gpt * p, q * LANES)

    aq = jnp.stack([jnp.stack([pr[q, 0], pi[q, 0]]), jnp.stack([pr[q, 1], pi[q, 1]])])
    aq = aq.reshape(2, 2, nt, gpt, p).transpose(2, 0, 1, 3, 4).reshape(nt, 1, 4 * gpt * p)
    return t_mat.astype(BF16), ws.astype(BF16), wy.astype(BF16), aq


def _s5_kernel(u_ref, t_ref, ws_ref, wy_ref, aq_ref, y_ref, s_ref):
    u = u_ref[0]
    s_ref[...] = _dot(u, ws_ref[0])
    rows, width = s_ref.shape
    hw = width // 4
    aq = aq_ref[0]
    afr, afi = aq[:, 0:hw], aq[:, hw:2 * hw]
    abr, abi = aq[:, 2 * hw:3 * hw], aq[:, 3 * hw:4 * hw]

    def body(c, carry):
        xfr, xfi, xbr, xbi = carry
        rf = pl.ds(c, 1)
        rb = pl.ds(rows - 1 - c, 1)
        sfr = s_ref[rf, 0:hw]
        sfi = s_ref[rf, hw:2 * hw]
        sbr = s_ref[rb, 2 * hw:3 * hw]
        sbi = s_ref[rb, 3 * hw:4 * hw]
        s_ref[rf, 0:hw] = xfr
        s_ref[rf, hw:2 * hw] = xfi
        s_ref[rb, 2 * hw:3 * hw] = xbr
        s_ref[rb, 3 * hw:4 * hw] = xbi
        return (afr * xfr - afi * xfi + sfr, afr * xfi + afi * xfr + sfi,
                abr * xbr - abi * xbi + sbr, abr * xbi + abi * xbr + sbi)

    zero = jnp.zeros((1, hw), F32)
    lax.fori_loop(0, rows, body, (zero, zero, zero, zero), unroll=8)
    y_ref[0] = _dot(u, t_ref[0]) + _dot(s_ref[...].astype(BF16), wy_ref[0])


def _s5(u_tiles, t_mat, ws, wy, aq, bn, q):
    nt, t, _ = u_tiles.shape
    rows = t // q // bn
    u2 = u_tiles.reshape(nt, t // q, q * LANES)
    sw = ws.shape[2]
    y2 = pl.pallas_call(
        _s5_kernel,
        grid=(nt, bn),
        in_specs=[
            pl.BlockSpec((1, rows, q * LANES), lambda j, b: (j, b, 0)),
            pl.BlockSpec((1,) + t_mat.shape[1:], lambda j, b: (j, 0, 0)),
            pl.BlockSpec((1,) + ws.shape[1:], lambda j, b: (j, 0, 0)),
            pl.BlockSpec((1,) + wy.shape[1:], lambda j, b: (j, 0, 0)),
            pl.BlockSpec((1, 1, sw), lambda j, b: (j, 0, 0)),
        ],
        out_specs=pl.BlockSpec((1, rows, q * LANES), lambda j, b: (j, b, 0)),
        out_shape=jax.ShapeDtypeStruct((nt, t // q, q * LANES), F32),
        scratch_shapes=[pltpu.VMEM((rows, sw), F32)],
        compiler_params=_cparams(("parallel", "parallel"), 48),
        name="s5_chunked",
    )(u2, t_mat, ws, wy, aq)
    return y2.reshape(nt, t, LANES)


def _l0_out_kernel(x_ref, mod_ref, a_ref, y5_ref, u_ref, d_ref, wglu_ref, wout_ref, o_ref, *, n_tiles):
    ys = []
    for j in range(n_tiles):
        ys.append(jax.nn.gelu(y5_ref[j] + d_ref[j] * u_ref[j].astype(F32)))
    y = jnp.concatenate(ys, axis=-1)
    glu = jax.nn.sigmoid(_dot(y.astype(BF16), wglu_ref[...]))
    b_out = (y * glu).astype(BF16)
    a_width = a_ref.shape[1]
    out = _dot(a_ref[...], wout_ref[0:a_width, :]) + _dot(b_out, wout_ref[a_width:, :])
    o_ref[...] = x_ref[...] + mod_ref[0, 2:3, :] * out


def _l0_out(x2, mod, a_out, y5, u_tiles, d_skip, w_glu, w_out, seq, tm):
    t, d = x2.shape
    n_tiles = u_tiles.shape[0]
    a_width = a_out.shape[1]
    kern = functools.partial(_l0_out_kernel, n_tiles=n_tiles)
    return pl.pallas_call(
        kern,
        grid=(t // tm,),
        in_specs=[
            pl.BlockSpec((tm, d), lambda i: (i, 0)),
            pl.BlockSpec((1, 6, d), _row_block(tm, seq)),
            pl.BlockSpec((tm, a_width), lambda i: (i, 0)),
            pl.BlockSpec((n_tiles, tm, LANES), lambda i: (0, i, 0)),
            pl.BlockSpec((n_tiles, tm, LANES), lambda i: (0, i, 0)),
            pl.BlockSpec((n_tiles, 1, LANES), lambda i: (0, 0, 0)),
            pl.BlockSpec(w_glu.shape, lambda i: (0, 0)),
            pl.BlockSpec(w_out.shape, lambda i: (0, 0)),
        ],
        out_specs=pl.BlockSpec((tm, d), lambda i: (i, 0)),
        out_shape=jax.ShapeDtypeStruct((t, d), F32),
        compiler_params=_cparams(("parallel",), 40),
        name="l0_out",
    )(x2, mod, a_out, y5, u_tiles, d_skip.reshape(n_tiles, 1, LANES),
      w_glu.astype(BF16), w_out.astype(BF16))


def _ffn_kernel(x_ref, mod_ref, g_ref, w1_ref, w3_ref, w2_ref, o_ref, h_scr, acc_scr):
    f = pl.program_id(1)

    @pl.when(f == 0)
    def _():
        h_scr[...] = _norm_mod(x_ref[...], g_ref[...], mod_ref[0, 3:4, :], mod_ref[0, 4:5, :]).astype(BF16)
        acc_scr[...] = jnp.zeros_like(acc_scr)

    h = h_scr[...]
    a = _dot(h, w1_ref[...])
    b = _dot(h, w3_ref[...])
    act = (a * jax.nn.sigmoid(a) * b).astype(BF16)
    acc_scr[...] += _dot(act, w2_ref[...])

    @pl.when(f == pl.num_programs(1) - 1)
    def _():
        o_ref[...] = x_ref[...] + mod_ref[0, 5:6, :] * acc_scr[...]


def _ffn(x2, mod, norm_g, w1, w3, w2, seq, tm, tf):
    t, d = x2.shape
    ff = w1.shape[1]
    return pl.pallas_call(
        _ffn_kernel,
        grid=(t // tm, ff // tf),
        in_specs=[
            pl.BlockSpec((tm, d), lambda i, f: (i, 0)),
            pl.BlockSpec((1, 6, d), _row_block(tm, seq)),
            pl.BlockSpec((1, d), lambda i, f: (0, 0)),
            pl.BlockSpec((d, tf), lambda i, f: (0, f)),
            pl.BlockSpec((d, tf), lambda i, f: (0, f)),
            pl.BlockSpec((tf, d), lambda i, f: (f, 0)),
        ],
        out_specs=pl.BlockSpec((tm, d), lambda i, f: (i, 0)),
        out_shape=jax.ShapeDtypeStruct((t, d), F32),
        scratch_shapes=[pltpu.VMEM((tm, d), BF16), pltpu.VMEM((tm, d), F32)],
        compiler_params=_cparams(("parallel", "arbitrary"), 48),
        name="ffn_swiglu",
    )(x2, mod, norm_g.reshape(1, d), w1.astype(BF16), w3.astype(BF16), w2.astype(BF16))


_NT_DIMS = (((1,), (1,)), ((), ()))
_TN_DIMS = (((0,), (0,)), ((), ()))


def _mla_proj_kernel(x_ref, mod_ref, g_ref, win_ref, gq_ref, gkv_ref, wqt_ref, wkn_ref, wvt_ref,
                     cos_ref, sin_ref, cost_ref, sint_ref, qt_ref, k_ref, vt_ref, *, q_lora, kv_lora, heads):
    h = _norm_mod(x_ref[...], g_ref[...], mod_ref[0, 0:1, :], mod_ref[0, 1:2, :]).astype(BF16)
    z = _dot(h, win_ref[...])
    cq = z[:, :q_lora]
    cq = (cq * lax.rsqrt(jnp.mean(cq * cq, axis=-1, keepdims=True) + RMS_EPS) * gq_ref[...]).astype(BF16)
    ckv = z[:, q_lora:q_lora + kv_lora]
    ckv = (ckv * lax.rsqrt(jnp.mean(ckv * ckv, axis=-1, keepdims=True) + RMS_EPS) * gkv_ref[...]).astype(BF16)
    r0 = q_lora + kv_lora
    k_rope = (z[:, r0:r0 + LANES] * cos_ref[...] + z[:, r0 + LANES:r0 + 2 * LANES] * sin_ref[...]).astype(BF16)
    kn = _dot(ckv, wkn_ref[...])
    qat = lax.dot_general(wqt_ref[...], cq, _NT_DIMS, preferred_element_type=F32)
    vt_ref[0] = lax.dot_general(wvt_ref[...], ckv, _NT_DIMS, preferred_element_type=F32).astype(BF16)
    cost = cost_ref[...]
    sint = sint_ref[...]
    hw = heads * LANES
    for hd in range(heads):
        c = slice(hd * LANES, (hd + 1) * LANES)
        qt_ref[2 * hd * LANES:(2 * hd + 1) * LANES, :] = qat[c, :].astype(BF16)
        q_rope = qat[hw + hd * LANES:hw + (hd + 1) * LANES, :] * cost \
            + qat[2 * hw + hd * LANES:2 * hw + (hd + 1) * LANES, :] * sint
        qt_ref[(2 * hd + 1) * LANES:(2 * hd + 2) * LANES, :] = q_rope.astype(BF16)
        k_ref[:, 2 * hd * LANES:(2 * hd + 1) * LANES] = kn[:, c].astype(BF16)
        k_ref[:, (2 * hd + 1) * LANES:(2 * hd + 2) * LANES] = k_rope


def _rope_pad(w_rope):
    half = QK_ROPE // 2
    x1, x2 = w_rope[..., :half], w_rope[..., half:]
    zeros = jnp.zeros(w_rope.shape[:-1] + (LANES - QK_ROPE,), w_rope.dtype)
    return jnp.concatenate([x1, x2, zeros], -1), jnp.concatenate([-x2, x1, zeros], -1)


def _mla_proj(x2, mod, norm_g, w_in, gq, gkv, w_uq, w_ukv, seq, tm):
    t, d = x2.shape
    q_lora, kv_lora = gq.shape[0], gkv.shape[0]
    heads = C_HEADS
    scale = math.log2(math.e) / math.sqrt(QK_NOPE + QK_ROPE)
    kr, krs = _rope_pad(w_in[:, q_lora + kv_lora:])
    w_in_ext = jnp.concatenate([w_in[:, :q_lora + kv_lora], kr, krs], -1).astype(BF16)
    wq3 = (w_uq * scale).reshape(q_lora, heads, QK_NOPE + QK_ROPE)
    qr, qrs = _rope_pad(wq3[..., QK_NOPE:])
    wqt = jnp.concatenate([wq3[..., :QK_NOPE].reshape(q_lora, -1), qr.reshape(q_lora, -1),
                           qrs.reshape(q_lora, -1)], -1).T.astype(BF16)
    wkv3 = w_ukv.reshape(kv_lora, heads, QK_NOPE + V_HEAD)
    wkn = wkv3[..., :QK_NOPE].reshape(kv_lora, -1).astype(BF16)
    wvt = wkv3[..., QK_NOPE:].reshape(kv_lora, -1).T.astype(BF16)
    inv_freq = ROPE_THETA ** (-jnp.arange(0, QK_ROPE, 2, dtype=F32) / QK_ROPE)
    ang = jnp.arange(seq, dtype=F32)[:, None] * inv_freq[None, :]
    pad = jnp.zeros((seq, LANES - QK_ROPE), F32)
    cos = jnp.concatenate([jnp.cos(ang), jnp.cos(ang), pad], -1)
    sin = jnp.concatenate([jnp.sin(ang), jnp.sin(ang), pad], -1)
    per_batch = seq // tm
    kern = functools.partial(_mla_proj_kernel, q_lora=q_lora, kv_lora=kv_lora, heads=heads)
    const = lambda i: (0, 0)
    return pl.pallas_call(
        kern,
        grid=(t // tm,),
        in_specs=[
            pl.BlockSpec((tm, d), lambda i: (i, 0)),
            pl.BlockSpec((1, 6, d), _row_block(tm, seq)),
            pl.BlockSpec((1, d), const),
            pl.BlockSpec(w_in_ext.shape, const),
            pl.BlockSpec((1, q_lora), const),
            pl.BlockSpec((1, kv_lora), const),
            pl.BlockSpec(wqt.shape, const),
            pl.BlockSpec(wkn.shape, const),
            pl.BlockSpec(wvt.shape, const),
            pl.BlockSpec((tm, LANES), lambda i: (i % per_batch, 0)),
            pl.BlockSpec((tm, LANES), lambda i: (i % per_batch, 0)),
            pl.BlockSpec((LANES, tm), lambda i: (0, i % per_batch)),
            pl.BlockSpec((LANES, tm), lambda i: (0, i % per_batch)),
        ],
        out_specs=[
            pl.BlockSpec((2 * heads * LANES, tm), lambda i: (0, i)),
            pl.BlockSpec((tm, 2 * heads * LANES), lambda i: (i, 0)),
            pl.BlockSpec((1, heads * V_HEAD, tm), lambda i: (i, 0, 0)),
        ],
        out_shape=[
            jax.ShapeDtypeStruct((2 * heads * LANES, t), BF16),
            jax.ShapeDtypeStruct((t, 2 * heads * LANES), BF16),
            jax.ShapeDtypeStruct((t // tm, heads * V_HEAD, tm), BF16),
        ],
        compiler_params=_cparams(("parallel",), 48),
        name="mla_proj",
    )(x2, mod, norm_g.reshape(1, d), w_in_ext, gq.reshape(1, -1), gkv.reshape(1, -1), wqt, wkn, wvt,
      cos, sin, cos.T, sin.T)


def _attn_kernel(qt_ref, k_ref, vt_ref, o_ref, s_scr, acc_scr, *, tk):
    tq = qt_ref.shape[1]
    n_kv = k_ref.shape[0] // tk
    tv = vt_ref.shape[2]
    sub = tk // tv

    def scores(c, slot):
        k = k_ref[pl.ds(pl.multiple_of(c * tk, tk), tk), :]
        s_scr[slot] = _dot(k, qt_ref[...])

    def update(c, slot, m, l):
        s = s_scr[slot]
        m_new = jnp.maximum(m, jnp.max(s, axis=0, keepdims=True))
        alpha = jnp.exp2(m - m_new)
        p = jnp.exp2(s - m_new)
        l = alpha * l + jnp.sum(p, axis=0, keepdims=True)
        pb = p.astype(BF16)
        acc = alpha * acc_scr[...]
        for j in range(sub):
            acc = acc + _dot(vt_ref[c * sub + j], pb[j * tv:(j + 1) * tv])
        acc_scr[...] = acc
        return m_new, l

    def body(c2, carry):
        m, l = carry
        c = 2 * c2
        scores(c + 1, 1)
        m, l = update(c, 0, m, l)
        scores(c + 2, 0)
        return update(c + 1, 1, m, l)

    acc_scr[...] = jnp.zeros_like(acc_scr)
    scores(0, 0)
    m, l = lax.fori_loop(0, n_kv // 2 - 1, body,
                         (jnp.full((1, tq), -jnp.inf, F32), jnp.zeros((1, tq), F32)))
    scores(n_kv - 1, 1)
    m, l = update(n_kv - 2, 0, m, l)
    m, l = update(n_kv - 1, 1, m, l)
    o_ref[0] = (acc_scr[...] / l).astype(BF16)


def _attention(qt, k, vt, bn, seq, tq, tk):
    tv = vt.shape[2]
    t = k.shape[0]
    heads = C_HEADS
    nq = seq // tq
    kern = functools.partial(_attn_kernel, tk=tk)
    return pl.pallas_call(
        kern,
        grid=(bn, heads, nq),
        in_specs=[
            pl.BlockSpec((2 * LANES, tq), lambda b, h, i: (h, b * nq + i)),
            pl.BlockSpec((seq, 2 * LANES), lambda b, h, i: (b, h)),
            pl.BlockSpec((seq // tv, V_HEAD, tv), lambda b, h, i: (b, h, 0)),
        ],
        out_specs=pl.BlockSpec((1, V_HEAD, tq), lambda b, h, i: (b * nq + i, h, 0)),
        out_shape=jax.ShapeDtypeStruct((t // tq, heads * V_HEAD, tq), BF16),
        scratch_shapes=[pltpu.VMEM((2, tk, tq), F32), pltpu.VMEM((V_HEAD, tq), F32)],
        compiler_params=_cparams(("parallel", "parallel", "parallel"), 48),
        name="mla_attention",
    )(qt, k, vt)


ROW_ALIGN = 16
SLAB_SIZES = (512, 256, 128, 64, 32, 16)


def _slab_copies(src_ref, src_row, dst_ref, dst_row, n_rows, sem, fn):
    done = 0
    for size in SLAB_SIZES:
        take = (n_rows & size) != 0
        s0 = pl.multiple_of(src_row + done, ROW_ALIGN)
        d0 = pl.multiple_of(dst_row + done, ROW_ALIGN)

        @pl.when(take)
        def _(s0=s0, d0=d0, size=size):
            fn(pltpu.make_async_copy(src_ref.at[pl.ds(s0, size), :], dst_ref.at[pl.ds(d0, size), :], sem))

        done = done + jnp.where(take, size, 0)


def _attn_out_kernel(x_ref, mod_ref, o_ref, wo_ref, g_ref, wr_ref, x3_ref, h_ref, info_ref, tbl_ref,
                     run_scr, *, n_experts):
    @pl.when(pl.program_id(0) == 0)
    def _():
        run_scr[...] = jnp.zeros_like(run_scr)

    y = lax.dot_general(o_ref[0], wo_ref[...], _TN_DIMS, preferred_element_type=F32)
    x3 = x_ref[...] + mod_ref[0, 2:3, :] * y
    x3_ref[...] = x3
    h = _norm_mod(x3, g_ref[...], mod_ref[0, 3:4, :], mod_ref[0, 4:5, :])
    h_ref[...] = h.astype(BF16)
    tm, d = h.shape

    logits = jnp.dot(h, wr_ref[...], preferred_element_type=F32, precision=HIGHEST)
    lane = lax.broadcasted_iota(jnp.int32, (tm, LANES), 1).astype(F32)
    neg = jnp.float32(-jnp.inf)
    lg = jnp.where(lane < n_experts, logits, neg)
    m1 = jnp.max(lg, axis=-1, keepdims=True)
    i1 = jnp.min(jnp.where(lg == m1, lane, float(LANES)), axis=-1, keepdims=True)
    lg2 = jnp.where(lane == i1, neg, lg)
    m2 = jnp.max(lg2, axis=-1, keepdims=True)
    i2 = jnp.min(jnp.where(lg2 == m2, lane, float(LANES)), axis=-1, keepdims=True)
    e = jnp.exp(m2 - m1)
    g0 = 1.0 / (1.0 + e)
    g1 = e / (1.0 + e)
    sel1 = lane == i1
    sel2 = lane == i2
    onehot = jnp.where(sel1, 1.0, 0.0) + jnp.where(sel2, 1.0, 0.0)
    row = lax.broadcasted_iota(jnp.int32, (tm, tm), 0)
    col = lax.broadcasted_iota(jnp.int32, (tm, tm), 1)
    tri = jnp.where(col < row, 1.0, 0.0).astype(BF16)
    before = _dot(tri, onehot.astype(BF16))
    n_pick = jnp.sum(onehot, axis=0, keepdims=True)
    n_slab = jnp.floor((n_pick + (ROW_ALIGN - 1)) * (1.0 / ROW_ALIGN)) * ROW_ALIGN
    lrow = lax.broadcasted_iota(jnp.int32, (LANES, LANES), 0)
    lcol = lax.broadcasted_iota(jnp.int32, (LANES, LANES), 1)
    upper = jnp.where(lrow < lcol, 1.0, 0.0).astype(BF16)
    n8 = jnp.broadcast_to(n_slab, (SUBLANES, LANES))
    loc = _dot(n8.astype(BF16), upper)[0:1]
    where = before + loc
    pos0 = jnp.sum(jnp.where(sel1, where, 0.0), axis=-1, keepdims=True)
    pos1 = jnp.sum(jnp.where(sel2, where, 0.0), axis=-1, keepdims=True)
    info_ref[...] = jnp.where(lane == 0, pos0, jnp.where(lane == 1, pos1, jnp.where(
        lane == 2, g0, jnp.where(lane == 3, g1, 0.0))))

    off = run_scr[...]
    run_scr[...] = off + n_slab
    trow = lax.broadcasted_iota(jnp.int32, (SUBLANES, LANES), 0)
    tbl = jnp.where(trow == 0, n_slab, jnp.where(trow == 1, off, jnp.where(trow == 2, loc, 0.0)))
    tbl_ref[0] = tbl.astype(jnp.int32)


def _attn_out(x2, mod, o, w_o, norm_g, w_router, seq, tm):
    t, d = x2.shape
    n_experts = w_router.shape[1]
    wr = jnp.zeros((d, LANES), F32).at[:, :n_experts].set(w_router)
    kern = functools.partial(_attn_out_kernel, n_experts=n_experts)
    const = lambda i: (0, 0)
    return pl.pallas_call(
        kern,
        grid=(t // tm,),
        in_specs=[
            pl.BlockSpec((tm, d), lambda i: (i, 0)),
            pl.BlockSpec((1, 6, d), _row_block(tm, seq)),
            pl.BlockSpec((1, o.shape[1], tm), lambda i: (i, 0, 0)),
            pl.BlockSpec(w_o.shape, const),
            pl.BlockSpec((1, d), const),
            pl.BlockSpec((d, LANES), const),
        ],
        out_specs=[
            pl.BlockSpec((tm, d), lambda i: (i, 0)),
            pl.BlockSpec((tm, d), lambda i: (i, 0)),
            pl.BlockSpec((tm, LANES), lambda i: (i, 0)),
            pl.BlockSpec((1, SUBLANES, LANES), lambda i: (i, 0, 0)),
        ],
        out_shape=[
            jax.ShapeDtypeStruct((t, d), F32),
            jax.ShapeDtypeStruct((t, d), BF16),
            jax.ShapeDtypeStruct((t, LANES), F32),
            jax.ShapeDtypeStruct((t // tm, SUBLANES, LANES), jnp.int32),
        ],
        scratch_shapes=[pltpu.VMEM((1, LANES), F32)],
        compiler_params=_cparams(("arbitrary",), 40),
        name="attn_out_router",
    )(x2, mod, o, w_o.astype(BF16), norm_g.reshape(1, d), wr)


def _dispatch_kernel(tbl_ref, h_ref, info_ref, xr_hbm, xs_scr, zero_scr, sem, *, n_experts, n_fill):
    i = pl.program_id(0)
    n_blk = pl.num_programs(0)
    tm = h_ref.shape[0]
    ns = xs_scr.shape[0]
    base = i * (3 * n_experts)
    starts = n_blk * (3 * n_experts)

    srow = lax.broadcasted_iota(jnp.int32, (tm, ns), 1).astype(F32)
    pick = jnp.logical_or(srow == info_ref[:, 0:1], srow == info_ref[:, 1:2])
    pick = jnp.where(pick, 1.0, 0.0).astype(BF16)
    xs_scr[...] = lax.dot_general(pick, h_ref[...], _TN_DIMS, preferred_element_type=F32).astype(BF16)

    def slabs(fn):
        for e in range(n_experts):
            dst = tbl_ref[starts + e] + tbl_ref[base + n_experts + e]
            _slab_copies(xs_scr, tbl_ref[base + 2 * n_experts + e], xr_hbm, dst, tbl_ref[base + e], sem, fn)

    slabs(lambda cp: cp.start())
    slabs(lambda cp: cp.wait())

    @pl.when(i == n_blk - 1)
    def _():
        zero_scr[...] = jnp.zeros_like(zero_scr)
        used = tbl_ref[starts + n_experts]

        def fills(fn):
            for e in range(n_experts):
                end = tbl_ref[starts + e] + tbl_ref[base + n_experts + e] + tbl_ref[base + e]
                _slab_copies(zero_scr, 0, xr_hbm, end, (-end) & (MOE_PAD - 1), sem, fn)
            for j in range(n_fill):
                row = pl.multiple_of(used + j * MOE_PAD, MOE_PAD)

                @pl.when(row < xr_hbm.shape[0])
                def _(row=row):
                    fn(pltpu.make_async_copy(zero_scr, xr_hbm.at[pl.ds(row, MOE_PAD), :], sem))

        fills(lambda cp: cp.start())
        fills(lambda cp: cp.wait())


def _dispatch(tbl, hb, info, tm, ns, n_rows, n_experts):
    t, d = hb.shape
    n_fill = n_rows // MOE_PAD - (TOP_K * t) // MOE_PAD
    kern = functools.partial(_dispatch_kernel, n_experts=n_experts, n_fill=n_fill)
    return pl.pallas_call(
        kern,
        grid_spec=pltpu.PrefetchScalarGridSpec(
            num_scalar_prefetch=1,
            grid=(t // tm,),
            in_specs=[
                pl.BlockSpec((tm, d), lambda i, tb: (i, 0)),
                pl.BlockSpec((tm, LANES), lambda i, tb: (i, 0)),
            ],
            out_specs=pl.BlockSpec(memory_space=pl.ANY),
            scratch_shapes=[pltpu.VMEM((ns, d), BF16), pltpu.VMEM((MOE_PAD, d), BF16),
                            pltpu.SemaphoreType.DMA(())],
        ),
        out_shape=jax.ShapeDtypeStruct((n_rows, d), BF16),
        compiler_params=_cparams(("arbitrary",), 40),
        name="moe_dispatch",
    )(tbl, hb, info)


def _experts_kernel(be_ref, na_ref, x_ref, w1_ref, w3_ref, w2_ref, o_ref, acc_scr):
    b = pl.program_id(0)
    f = pl.program_id(1)
    active = b < na_ref[0]
    last = f == pl.num_programs(1) - 1

    @pl.when(jnp.logical_and(active, f == 0))
    def _():
        acc_scr[...] = jnp.zeros_like(acc_scr)

    @pl.when(active)
    def _():
        x = x_ref[...]
        a = _dot(x, w1_ref[0])
        g = _dot(x, w3_ref[0])
        act = (a * jax.nn.sigmoid(a) * g).astype(BF16)
        acc_scr[...] += _dot(act, w2_ref[0])

    @pl.when(jnp.logical_and(active, last))
    def _():
        o_ref[...] = acc_scr[...].astype(BF16)

    @pl.when(jnp.logical_and(jnp.logical_not(active), last))
    def _():
        o_ref[...] = jnp.zeros_like(o_ref)


def _experts(blk_expert, n_active, xr, w1, w3, w2, tf):
    rows = MOE_PAD
    _, d, ff = w1.shape
    last_f = ff // tf - 1

    def ftile(b, f, na):
        return jnp.where(b < na[0], f, last_f)

    return pl.pallas_call(
        _experts_kernel,
        grid_spec=pltpu.PrefetchScalarGridSpec(
            num_scalar_prefetch=2,
            grid=(xr.shape[0] // rows, ff // tf),
            in_specs=[
                pl.BlockSpec((rows, d), lambda b, f, be, na: (b, 0)),
                pl.BlockSpec((1, d, tf), lambda b, f, be, na: (be[b], 0, ftile(b, f, na))),
                pl.BlockSpec((1, d, tf), lambda b, f, be, na: (be[b], 0, ftile(b, f, na))),
                pl.BlockSpec((1, tf, d), lambda b, f, be, na: (be[b], ftile(b, f, na), 0)),
            ],
            out_specs=pl.BlockSpec((rows, d), lambda b, f, be, na: (b, 0)),
            scratch_shapes=[pltpu.VMEM((rows, d), F32)],
        ),
        out_shape=jax.ShapeDtypeStruct(xr.shape, BF16),
        compiler_params=_cparams(("parallel", "arbitrary"), 48),
        name="moe_experts",
    )(blk_expert, n_active, xr, w1, w3, w2)


def _combine_kernel(tbl_ref, x_ref, mod_ref, info_ref, fg_ref, y_hbm, o_ref, ys_scr, sem, *, n_experts):
    tm, d = x_ref.shape
    ns = ys_scr.shape[0]
    base = pl.program_id(0) * (3 * n_experts)
    starts = pl.num_programs(0) * (3 * n_experts)
    ys_scr[...] = jnp.zeros_like(ys_scr)

    def slabs(fn):
        for e in range(n_experts):
            src = tbl_ref[starts + e] + tbl_ref[base + n_experts + e]
            _slab_copies(y_hbm, src, ys_scr, tbl_ref[base + 2 * n_experts + e], tbl_ref[base + e], sem, fn)

    slabs(lambda cp: cp.start())
    slabs(lambda cp: cp.wait())

    ys = ys_scr[...]
    srow = lax.broadcasted_iota(jnp.int32, (tm, ns), 1).astype(F32)
    y0 = _dot(jnp.where(srow == info_ref[:, 0:1], 1.0, 0.0).astype(BF16), ys)
    y1 = _dot(jnp.where(srow == info_ref[:, 1:2], 1.0, 0.0).astype(BF16), ys)
    y = info_ref[:, 2:3] * y0 + info_ref[:, 3:4] * y1
    x4 = x_ref[...] + mod_ref[0, 5:6, :] * y
    o_ref[...] = x4 * lax.rsqrt(jnp.mean(x4 * x4, axis=-1, keepdims=True) + RMS_EPS) * fg_ref[...]


def _combine(tbl, x3, mod, info, final_g, yr, seq, tm, ns, n_experts):
    t, d = x3.shape
    kern = functools.partial(_combine_kernel, n_experts=n_experts)
    return pl.pallas_call(
        kern,
        grid_spec=pltpu.PrefetchScalarGridSpec(
            num_scalar_prefetch=1,
            grid=(t // tm,),
            in_specs=[
                pl.BlockSpec((tm, d), lambda i, tb: (i, 0)),
                pl.BlockSpec((1, 6, d), _row_block(tm, seq)),
                pl.BlockSpec((tm, LANES), lambda i, tb: (i, 0)),
                pl.BlockSpec((1, d), lambda i, tb: (0, 0)),
                pl.BlockSpec(memory_space=pl.ANY),
            ],
            out_specs=pl.BlockSpec((tm, d), lambda i, tb: (i, 0)),
            scratch_shapes=[pltpu.VMEM((ns, d), BF16), pltpu.SemaphoreType.DMA(())],
        ),
        out_shape=jax.ShapeDtypeStruct((t, d), F32),
        compiler_params=_cparams(("arbitrary",), 40),
        name="moe_combine_final",
    )(tbl, x3, mod, info, final_g.reshape(1, d), yr)


def _moe(x3, mod, hb, info, tbl, w1, w3, w2, final_g, seq, tm):
    t, d = x3.shape
    n_experts = w1.shape[0]
    n_blk = t // tm
    ns = -(-(TOP_K * tm + n_experts * (ROW_ALIGN - 1)) // LANES) * LANES
    steps = (TOP_K * t + n_experts * n_blk * (ROW_ALIGN - 1)) // MOE_PAD + n_experts
    end = tbl[-1, 1, :n_experts] + tbl[-1, 0, :n_experts]
    nb = (end + MOE_PAD - 1) // MOE_PAD
    cum = jnp.cumsum(nb)
    n_active = cum[-1:].astype(jnp.int32)
    step = jnp.arange(steps, dtype=jnp.int32)
    blk_expert = jnp.minimum(jnp.searchsorted(cum, jnp.minimum(step, n_active[0] - 1), side='right'),
                             n_experts - 1).astype(jnp.int32)
    flat = jnp.concatenate([tbl[:, :3, :n_experts].reshape(-1), (cum - nb) * MOE_PAD, n_active * MOE_PAD])
    flat = flat.astype(jnp.int32)
    xr = _dispatch(flat, hb, info, tm, ns, steps * MOE_PAD, n_experts)
    yr = _experts(blk_expert, n_active, xr, w1.astype(BF16), w3.astype(BF16), w2.astype(BF16),
                  w1.shape[2] // 2)
    return _combine(flat, x3, mod, info, final_g, yr, seq, tm, ns, n_experts)


def kernel(x, c, ada_w, ada_b, norm_mix_g, norm_ffn_g, final_g, ab_w_in, sgu_v_g, sgu_w_s, sgu_b_s, s5_lam_re, s5_lam_im, s5_log_dt, s5_b_re, s5_b_im, s5_c_re, s5_c_im, s5_d, s5_w_glu, ab_w_out, ffn_w1, ffn_w3, ffn_w2, mla_w_in, mla_q_norm_g, mla_kv_norm_g, mla_w_uq, mla_w_ukv, mla_w_o, moe_w_router, moe_w1, moe_w3, moe_w2):
    bn, seq, d = x.shape
    t = bn * seq
    tm = min(512, seq)
    x2 = x.reshape(t, d)
    mod = _adaln(c, ada_w, ada_b)

    a_out, u_tiles = _l0_in(x2, mod[0], norm_mix_g[0], ab_w_in[0], sgu_w_s[0], sgu_b_s[0], sgu_v_g[0], seq, tm)
    t_mat, ws, wy, aq = _s5_weights(s5_lam_re[0], s5_lam_im[0], s5_log_dt[0], s5_b_re[0], s5_b_im[0],
                                    s5_c_re[0], s5_c_im[0], S5_Q)
    y5 = _s5(u_tiles, t_mat, ws, wy, aq, bn, S5_Q)
    x2 = _l0_out(x2, mod[0], a_out, y5, u_tiles, s5_d[0], s5_w_glu[0], ab_w_out[0], seq, tm)
    x2 = _ffn(x2, mod[0], norm_ffn_g[0], ffn_w1[0], ffn_w3[0], ffn_w2[0], seq, tm, ffn_w1.shape[2] // 2)

    qt, k, vt = _mla_proj(x2, mod[1], norm_mix_g[1], mla_w_in[0], mla_q_norm_g[0], mla_kv_norm_g[0],
                          mla_w_uq[0], mla_w_ukv[0], seq, tm)
    o = _attention(qt, k, vt, bn, seq, tm, min(1024, seq // 2))
    x3, hb, info, tbl = _attn_out(x2, mod[1], o, mla_w_o[0], norm_ffn_g[1], moe_w_router[0], seq, tm)
    out = _moe(x3, mod[1], hb, info, tbl, moe_w1[0], moe_w3[0], moe_w2[0], final_g, seq, tm)
    return out.reshape(bn, seq, d)
```

```python
import functools
import math

import jax
import jax.numpy as jnp
from jax import lax
from jax.experimental import pallas as pl
from jax.experimental.pallas import tpu as pltpu

F32 = jnp.float32
BF16 = jnp.bfloat16
HIGHEST = lax.Precision.HIGHEST

LANES = 128
SUBLANES = 8
MIB = 1024 * 1024

RMS_EPS = 1e-6
C_HEADS = 8
QK_NOPE = 128
QK_ROPE = 64
V_HEAD = 128
ROPE_THETA = 10000.0
TOP_K = 2

S5_Q = 8
MOE_PAD = 512


def _cparams(semantics, vmem_mib):
    return pltpu.CompilerParams(dimension_semantics=semantics, vmem_limit_bytes=vmem_mib * MIB)


def _dot(a, b):
    return jnp.dot(a, b, preferred_element_type=F32)


def _norm_mod(x, g, shift, scale):
    y = x * lax.rsqrt(jnp.mean(x * x, axis=-1, keepdims=True) + RMS_EPS)
    return (y * g) * (1.0 + scale) + shift


def _row_block(tm, seq):
    per_batch = seq // tm
    return lambda i, *_: (i // per_batch, 0, 0)


def _adaln_kernel(c_ref, w_ref, b_ref, o_ref):
    c = c_ref[...]
    ca = c * jax.nn.sigmoid(c)
    o_ref[0] = jnp.dot(ca, w_ref[0], preferred_element_type=F32, precision=HIGHEST) + b_ref[0]


def _adaln(c, ada_w, ada_b):
    depth, d, n = ada_w.shape
    bn = c.shape[0]
    tn = n // 4
    c8 = jnp.zeros((SUBLANES, d), F32).at[:bn].set(c)
    out = pl.pallas_call(
        _adaln_kernel,
        grid=(depth, n // tn),
        in_specs=[
            pl.BlockSpec((SUBLANES, d), lambda l, j: (0, 0)),
            pl.BlockSpec((1, d, tn), lambda l, j: (l, 0, j)),
            pl.BlockSpec((1, 1, tn), lambda l, j: (l, 0, j)),
        ],
        out_specs=pl.BlockSpec((1, SUBLANES, tn), lambda l, j: (l, 0, j)),
        out_shape=jax.ShapeDtypeStruct((depth, SUBLANES, n), F32),
        compiler_params=_cparams(("parallel", "parallel"), 32),
        name="adaln",
    )(c8, ada_w, ada_b.reshape(depth, 1, n))
    return out[:, :bn].reshape(depth, bn, 6, d)


def _l0_in_kernel(x_ref, mod_ref, g_ref, win_ref, ws_ref, bs_ref, vg_ref, a_ref, u_ref, *,
                  heads, chunk, n_tiles):
    x = x_ref[...]
    h = _norm_mod(x, g_ref[...], mod_ref[0, 0:1, :], mod_ref[0, 1:2, :]).astype(BF16)
    z = _dot(h, win_ref[...])
    tm = x.shape[0]
    a_width = heads * LANES
    for g in range(heads):
        u = jax.nn.gelu(z[:, g * LANES:(g + 1) * LANES])
        v = jax.nn.gelu(z[:, a_width + g * LANES:a_width + (g + 1) * LANES])
        vn = v * lax.rsqrt(jnp.mean(v * v, axis=-1, keepdims=True) + RMS_EPS) * vg_ref[g:g + 1, :]
        vnb = vn.astype(BF16)
        bias = bs_ref[:, g:g + 1]
        for ci in range(tm // chunk):
            rows = slice(ci * chunk, (ci + 1) * chunk)
            s = _dot(ws_ref[g], vnb[rows]) + bias
            a_ref[rows, g * LANES:(g + 1) * LANES] = (u[rows] * s).astype(BF16)
    for j in range(n_tiles):
        u_ref[j] = z[:, 2 * a_width + j * LANES:2 * a_width + (j + 1) * LANES].astype(BF16)


def _l0_in(x2, mod, norm_g, w_in, w_s, b_s, v_g, seq, tm):
    t, d = x2.shape
    heads, chunk, _ = w_s.shape
    a_width = heads * LANES
    b_width = w_in.shape[1] - 2 * a_width
    n_tiles = b_width // LANES
    kern = functools.partial(_l0_in_kernel, heads=heads, chunk=chunk, n_tiles=n_tiles)
    return pl.pallas_call(
        kern,
        grid=(t // tm,),
        in_specs=[
            pl.BlockSpec((tm, d), lambda i: (i, 0)),
            pl.BlockSpec((1, 6, d), _row_block(tm, seq)),
            pl.BlockSpec((1, d), lambda i: (0, 0)),
            pl.BlockSpec(w_in.shape, lambda i: (0, 0)),
            pl.BlockSpec(w_s.shape, lambda i: (0, 0, 0)),
            pl.BlockSpec((chunk, heads), lambda i: (0, 0)),
            pl.BlockSpec((heads, LANES), lambda i: (0, 0)),
        ],
        out_specs=[
            pl.BlockSpec((tm, a_width), lambda i: (i, 0)),
            pl.BlockSpec((n_tiles, tm, LANES), lambda i: (0, i, 0)),
        ],
        out_shape=[
            jax.ShapeDtypeStruct((t, a_width), BF16),
            jax.ShapeDtypeStruct((n_tiles, t, LANES), BF16),
        ],
        compiler_params=_cparams(("parallel",), 40),
        name="l0_in_sgu",
    )(x2, mod, norm_g.reshape(1, d), w_in.astype(BF16), w_s.astype(BF16), b_s.T, v_g)


def _s5_weights(lam_re, lam_im, log_dt, b_re, b_im, c_re, c_im, q):
    _, groups, p = lam_re.shape
    h = b_re.shape[-1]
    gpt = LANES // h
    nt = groups // gpt
    dt = jnp.exp(log_dt.astype(F32))[..., None]
    mag = jnp.exp(lam_re * dt)
    ang = lam_im * dt
    ar, ai = mag * jnp.cos(ang), mag * jnp.sin(ang)
    den = lam_re * lam_re + lam_im * lam_im
    fr = ((ar - 1.0) * lam_re + ai * lam_im) / den
    fi = (ai * lam_re - (ar - 1.0) * lam_im) / den
    bbr = fr[..., None] * b_re - fi[..., None] * b_im
    bbi = fr[..., None] * b_im + fi[..., None] * b_re
    prs, pis = [jnp.ones_like(ar)], [jnp.zeros_like(ar)]
    for _ in range(q):
        prs.append(prs[-1] * ar - pis[-1] * ai)
        pis.append(prs[-2] * ai + pis[-1] * ar)
    pr, pi = jnp.stack(prs), jnp.stack(pis)
    abr = pr[..., None] * bbr - pi[..., None] * bbi
    abi = pr[..., None] * bbi + pi[..., None] * bbr
    car = c_re * pr[:, :, :, None, :] - c_im * pi[:, :, :, None, :]
    cai = c_re * pi[:, :, :, None, :] + c_im * pr[:, :, :, None, :]
    kk = (jnp.einsum('dgop,kdgpi->kdgoi', c_re, abr[:q], precision=HIGHEST)
          - jnp.einsum('dgop,kdgpi->kdgoi', c_im, abi[:q], precision=HIGHEST))
    idx = jnp.arange(q)
    dfi = idx[None, :] - idx[:, None]
    kf = jnp.where((dfi >= 0)[:, :, None, None, None], kk[:, 0][jnp.clip(dfi, 0, q - 1)], 0.0)
    kb = jnp.where((dfi <= 0)[:, :, None, None, None], kk[:, 1][jnp.clip(-dfi, 0, q - 1)], 0.0)
    m = (kf + kb).reshape(q, q, nt, gpt, h, h)
    w4 = jnp.stack([jnp.stack([abr[q - 1 - idx, 0], abi[q - 1 - idx, 0]]),
                    jnp.stack([abr[idx, 1], abi[idx, 1]])])
    w4 = w4.reshape(2, 2, q, nt, gpt, p, h)
    y4 = jnp.stack([jnp.stack([car[idx + 1, 0], -cai[idx + 1, 0]]),
                    jnp.stack([car[q - idx, 1], -cai[q - idx, 1]])])
    y4 = y4.reshape(2, 2, q, nt, gpt, h, p)

    a_in = m.transpose(2, 0, 3, 5, 1, 4).reshape(nt, q * LANES, q * h)
    a_ws = w4.transpose(3, 2, 4, 6, 0, 1, 5).reshape(nt, q * LANES, 4 * p)
    a_wy = y4.transpose(3, 0, 1, 4, 6, 2, 5).reshape(nt, 4 * gpt * p, q * h)

    def expand(a, col_src, row_group, col_group):
        pick = (jnp.arange(a.shape[2])[:, None] == col_src[None, :]).astype(BF16)
        wide = jnp.einsum('jrk,kc->jrc', a.astype(BF16), pick, preferred_element_type=F32)
        return jnp.where(row_group[:, None] == col_group[None, :], wide, 0.0).astype(BF16)

    r_in = jnp.arange(q * LANES)
    r_st = jnp.arange(4 * gpt * p)
    g_in = (r_in // h) % gpt
    g_st = (r_st // p) % gpt
    src_in = (r_in // LANES) * h + r_in % h
    src_st = (r_st // (gpt * p)) * p + r_st % p
    t_mat = expand(a_in, src_in, g_in, g_in)
    ws = expand(a_ws, src_st, g_in, g_st)
    wy = expand(a_wy, src_in, g_st, g_in)

    aq = jnp.stack([jnp.stack([pr[q, 0], pi[q, 0]]), jnp.stack([pr[q, 1], pi[q, 1]])])
    aq = aq.reshape(2, 2, nt, gpt, p).transpose(2, 0, 1, 3, 4).reshape(nt, 1, 4 * gpt * p)
    return t_mat, ws, wy, aq


def _s5_kernel(u_ref, t_ref, ws_ref, wy_ref, aq_ref, y_ref, s_ref):
    u = u_ref[0]
    s_ref[...] = _dot(u, ws_ref[0])
    rows, width = s_ref.shape
    hw = width // 4
    aq = aq_ref[0]
    afr, afi = aq[:, 0:hw], aq[:, hw:2 * hw]
    abr, abi = aq[:, 2 * hw:3 * hw], aq[:, 3 * hw:4 * hw]

    def body(c, carry):
        xfr, xfi, xbr, xbi = carry
        rf = pl.ds(c, 1)
        rb = pl.ds(rows - 1 - c, 1)
        sfr = s_ref[rf, 0:hw]
        sfi = s_ref[rf, hw:2 * hw]
        sbr = s_ref[rb, 2 * hw:3 * hw]
        sbi = s_ref[rb, 3 * hw:4 * hw]
        s_ref[rf, 0:hw] = xfr
        s_ref[rf, hw:2 * hw] = xfi
        s_ref[rb, 2 * hw:3 * hw] = xbr
        s_ref[rb, 3 * hw:4 * hw] = xbi
        return (afr * xfr - afi * xfi + sfr, afr * xfi + afi * xfr + sfi,
                abr * xbr - abi * xbi + sbr, abr * xbi + abi * xbr + sbi)

    zero = jnp.zeros((1, hw), F32)
    lax.fori_loop(0, rows, body, (zero, zero, zero, zero), unroll=8)
    y_ref[0] = _dot(u, t_ref[0]) + _dot(s_ref[...].astype(BF16), wy_ref[0])


def _s5(u_tiles, t_mat, ws, wy, aq, bn, q):
    nt, t, _ = u_tiles.shape
    rows = t // q // bn
    u2 = u_tiles.reshape(nt, t // q, q * LANES)
    sw = ws.shape[2]
    y2 = pl.pallas_call(
        _s5_kernel,
        grid=(nt, bn),
        in_specs=[
            pl.BlockSpec((1, rows, q * LANES), lambda j, b: (j, b, 0)),
            pl.BlockSpec((1,) + t_mat.shape[1:], lambda j, b: (j, 0, 0)),
            pl.BlockSpec((1,) + ws.shape[1:], lambda j, b: (j, 0, 0)),
            pl.BlockSpec((1,) + wy.shape[1:], lambda j, b: (j, 0, 0)),
            pl.BlockSpec((1, 1, sw), lambda j, b: (j, 0, 0)),
        ],
        out_specs=pl.BlockSpec((1, rows, q * LANES), lambda j, b: (j, b, 0)),
        out_shape=jax.ShapeDtypeStruct((nt, t // q, q * LANES), F32),
        scratch_shapes=[pltpu.VMEM((rows, sw), F32)],
        compiler_params=_cparams(("parallel", "parallel"), 48),
        name="s5_chunked",
    )(u2, t_mat, ws, wy, aq)
    return y2.reshape(nt, t, LANES)


def _l0_out_kernel(x_ref, mod_ref, a_ref, y5_ref, u_ref, d_ref, wglu_ref, wout_ref, o_ref, *, n_tiles):
    ys = []
    for j in range(n_tiles):
        ys.append(jax.nn.gelu(y5_ref[j] + d_ref[j] * u_ref[j].astype(F32)))
    y = jnp.concatenate(ys, axis=-1)
    glu = jax.nn.sigmoid(_dot(y.astype(BF16), wglu_ref[...]))
    b_out = (y * glu).astype(BF16)
    a_width = a_ref.shape[1]
    out = _dot(a_ref[...], wout_ref[0:a_width, :]) + _dot(b_out, wout_ref[a_width:, :])
    o_ref[...] = x_ref[...] + mod_ref[0, 2:3, :] * out


def _l0_out(x2, mod, a_out, y5, u_tiles, d_skip, w_glu, w_out, seq, tm):
    t, d = x2.shape
    n_tiles = u_tiles.shape[0]
    a_width = a_out.shape[1]
    kern = functools.partial(_l0_out_kernel, n_tiles=n_tiles)
    return pl.pallas_call(
        kern,
        grid=(t // tm,),
        in_specs=[
            pl.BlockSpec((tm, d), lambda i: (i, 0)),
            pl.BlockSpec((1, 6, d), _row_block(tm, seq)),
            pl.BlockSpec((tm, a_width), lambda i: (i, 0)),
            pl.BlockSpec((n_tiles, tm, LANES), lambda i: (0, i, 0)),
            pl.BlockSpec((n_tiles, tm, LANES), lambda i: (0, i, 0)),
            pl.BlockSpec((n_tiles, 1, LANES), lambda i: (0, 0, 0)),
            pl.BlockSpec(w_glu.shape, lambda i: (0, 0)),
            pl.BlockSpec(w_out.shape, lambda i: (0, 0)),
        ],
        out_specs=pl.BlockSpec((tm, d), lambda i: (i, 0)),
        out_shape=jax.ShapeDtypeStruct((t, d), F32),
        compiler_params=_cparams(("parallel",), 40),
        name="l0_out",
    )(x2, mod, a_out, y5, u_tiles, d_skip.reshape(n_tiles, 1, LANES),
      w_glu.astype(BF16), w_out.astype(BF16))


def _ffn_kernel(x_ref, mod_ref, g_ref, w1_ref, w3_ref, w2_ref, o_ref, h_scr, acc_scr):
    f = pl.program_id(1)

    @pl.when(f == 0)
    def _():
        h_scr[...] = _norm_mod(x_ref[...], g_ref[...], mod_ref[0, 3:4, :], mod_ref[0, 4:5, :]).astype(BF16)
        acc_scr[...] = jnp.zeros_like(acc_scr)

    h = h_scr[...]
    a = _dot(h, w1_ref[...])
    b = _dot(h, w3_ref[...])
    act = (a * jax.nn.sigmoid(a) * b).astype(BF16)
    acc_scr[...] += _dot(act, w2_ref[...])

    @pl.when(f == pl.num_programs(1) - 1)
    def _():
        o_ref[...] = x_ref[...] + mod_ref[0, 5:6, :] * acc_scr[...]


def _ffn(x2, mod, norm_g, w1, w3, w2, seq, tm, tf):
    t, d = x2.shape
    ff = w1.shape[1]
    return pl.pallas_call(
        _ffn_kernel,
        grid=(t // tm, ff // tf),
        in_specs=[
            pl.BlockSpec((tm, d), lambda i, f: (i, 0)),
            pl.BlockSpec((1, 6, d), _row_block(tm, seq)),
            pl.BlockSpec((1, d), lambda i, f: (0, 0)),
            pl.BlockSpec((d, tf), lambda i, f: (0, f)),
            pl.BlockSpec((d, tf), lambda i, f: (0, f)),
            pl.BlockSpec((tf, d), lambda i, f: (f, 0)),
        ],
        out_specs=pl.BlockSpec((tm, d), lambda i, f: (i, 0)),
        out_shape=jax.ShapeDtypeStruct((t, d), F32),
        scratch_shapes=[pltpu.VMEM((tm, d), BF16), pltpu.VMEM((tm, d), F32)],
        compiler_params=_cparams(("parallel", "arbitrary"), 48),
        name="ffn_swiglu",
    )(x2, mod, norm_g.reshape(1, d), w1.astype(BF16), w3.astype(BF16), w2.astype(BF16))


_NT_DIMS = (((1,), (1,)), ((), ()))
_TN_DIMS = (((0,), (0,)), ((), ()))


def _mla_proj_kernel(x_ref, mod_ref, g_ref, win_ref, gq_ref, gkv_ref, wqt_ref, wkn_ref, wvt_ref,
                     cos_ref, sin_ref, cost_ref, sint_ref, qt_ref, k_ref, vt_ref, *, q_lora, kv_lora, heads):
    h = _norm_mod(x_ref[...], g_ref[...], mod_ref[0, 0:1, :], mod_ref[0, 1:2, :]).astype(BF16)
    z = _dot(h, win_ref[...])
    cq = z[:, :q_lora]
    cq = (cq * lax.rsqrt(jnp.mean(cq * cq, axis=-1, keepdims=True) + RMS_EPS) * gq_ref[...]).astype(BF16)
    ckv = z[:, q_lora:q_lora + kv_lora]
    ckv = (ckv * lax.rsqrt(jnp.mean(ckv * ckv, axis=-1, keepdims=True) + RMS_EPS) * gkv_ref[...]).astype(BF16)
    r0 = q_lora + kv_lora
    k_rope = (z[:, r0:r0 + LANES] * cos_ref[...] + z[:, r0 + LANES:r0 + 2 * LANES] * sin_ref[...]).astype(BF16)
    kn = _dot(ckv, wkn_ref[...])
    qat = lax.dot_general(wqt_ref[...], cq, _NT_DIMS, preferred_element_type=F32)
    vt_ref[0] = lax.dot_general(wvt_ref[...], ckv, _NT_DIMS, preferred_element_type=F32).astype(BF16)
    cost = cost_ref[...]
    sint = sint_ref[...]
    hw = heads * LANES
    for hd in range(heads):
        c = slice(hd * LANES, (hd + 1) * LANES)
        qt_ref[2 * hd * LANES:(2 * hd + 1) * LANES, :] = qat[c, :].astype(BF16)
        q_rope = qat[hw + hd * LANES:hw + (hd + 1) * LANES, :] * cost \
            + qat[2 * hw + hd * LANES:2 * hw + (hd + 1) * LANES, :] * sint
        qt_ref[(2 * hd + 1) * LANES:(2 * hd + 2) * LANES, :] = q_rope.astype(BF16)
        k_ref[:, 2 * hd * LANES:(2 * hd + 1) * LANES] = kn[:, c].astype(BF16)
        k_ref[:, (2 * hd + 1) * LANES:(2 * hd + 2) * LANES] = k_rope


def _rope_pad(w_rope):
    half = QK_ROPE // 2
    x1, x2 = w_rope[..., :half], w_rope[..., half:]
    zeros = jnp.zeros(w_rope.shape[:-1] + (LANES - QK_ROPE,), w_rope.dtype)
    return jnp.concatenate([x1, x2, zeros], -1), jnp.concatenate([-x2, x1, zeros], -1)


def _mla_proj(x2, mod, norm_g, w_in, gq, gkv, w_uq, w_ukv, seq, tm):
    t, d = x2.shape
    q_lora, kv_lora = gq.shape[0], gkv.shape[0]
    heads = C_HEADS
    scale = math.log2(math.e) / math.sqrt(QK_NOPE + QK_ROPE)
    kr, krs = _rope_pad(w_in[:, q_lora + kv_lora:])
    w_in_ext = jnp.concatenate([w_in[:, :q_lora + kv_lora], kr, krs], -1).astype(BF16)
    wq3 = (w_uq * scale).reshape(q_lora, heads, QK_NOPE + QK_ROPE)
    qr, qrs = _rope_pad(wq3[..., QK_NOPE:])
    wqt = jnp.concatenate([wq3[..., :QK_NOPE].reshape(q_lora, -1), qr.reshape(q_lora, -1),
                           qrs.reshape(q_lora, -1)], -1).T.astype(BF16)
    wkv3 = w_ukv.reshape(kv_lora, heads, QK_NOPE + V_HEAD)
    wkn = wkv3[..., :QK_NOPE].reshape(kv_lora, -1).astype(BF16)
    wvt = wkv3[..., QK_NOPE:].reshape(kv_lora, -1).T.astype(BF16)
    inv_freq = ROPE_THETA ** (-jnp.arange(0, QK_ROPE, 2, dtype=F32) / QK_ROPE)
    ang = jnp.arange(seq, dtype=F32)[:, None] * inv_freq[None, :]
    pad = jnp.zeros((seq, LANES - QK_ROPE), F32)
    cos = jnp.concatenate([jnp.cos(ang), jnp.cos(ang), pad], -1)
    sin = jnp.concatenate([jnp.sin(ang), jnp.sin(ang), pad], -1)
    per_batch = seq // tm
    kern = functools.partial(_mla_proj_kernel, q_lora=q_lora, kv_lora=kv_lora, heads=heads)
    const = lambda i: (0, 0)
    return pl.pallas_call(
        kern,
        grid=(t // tm,),
        in_specs=[
            pl.BlockSpec((tm, d), lambda i: (i, 0)),
            pl.BlockSpec((1, 6, d), _row_block(tm, seq)),
            pl.BlockSpec((1, d), const),
            pl.BlockSpec(w_in_ext.shape, const),
            pl.BlockSpec((1, q_lora), const),
            pl.BlockSpec((1, kv_lora), const),
            pl.BlockSpec(wqt.shape, const),
            pl.BlockSpec(wkn.shape, const),
            pl.BlockSpec(wvt.shape, const),
            pl.BlockSpec((tm, LANES), lambda i: (i % per_batch, 0)),
            pl.BlockSpec((tm, LANES), lambda i: (i % per_batch, 0)),
            pl.BlockSpec((LANES, tm), lambda i: (0, i % per_batch)),
            pl.BlockSpec((LANES, tm), lambda i: (0, i % per_batch)),
        ],
        out_specs=[
            pl.BlockSpec((2 * heads * LANES, tm), lambda i: (0, i)),
            pl.BlockSpec((tm, 2 * heads * LANES), lambda i: (i, 0)),
            pl.BlockSpec((1, heads * V_HEAD, tm), lambda i: (i, 0, 0)),
        ],
        out_shape=[
            jax.ShapeDtypeStruct((2 * heads * LANES, t), BF16),
            jax.ShapeDtypeStruct((t, 2 * heads * LANES), BF16),
            jax.ShapeDtypeStruct((t // tm, heads * V_HEAD, tm), BF16),
        ],
        compiler_params=_cparams(("parallel",), 48),
        name="mla_proj",
    )(x2, mod, norm_g.reshape(1, d), w_in_ext, gq.reshape(1, -1), gkv.reshape(1, -1), wqt, wkn, wvt,
      cos, sin, cos.T, sin.T)


def _attn_kernel(qt_ref, k_ref, vt_ref, o_ref, s_scr, acc_scr, *, tk):
    tq = qt_ref.shape[1]
    n_kv = k_ref.shape[0] // tk
    tv = vt_ref.shape[2]
    sub = tk // tv

    def scores(c, slot):
        k = k_ref[pl.ds(pl.multiple_of(c * tk, tk), tk), :]
        s_scr[slot] = _dot(k, qt_ref[...])

    def update(c, slot, m, l):
        s = s_scr[slot]
        m_new = jnp.maximum(m, jnp.max(s, axis=0, keepdims=True))
        alpha = jnp.exp2(m - m_new)
        p = jnp.exp2(s - m_new)
        l = alpha * l + jnp.sum(p, axis=0, keepdims=True)
        pb = p.astype(BF16)
        acc = alpha * acc_scr[...]
        for j in range(sub):
            acc = acc + _dot(vt_ref[c * sub + j], pb[j * tv:(j + 1) * tv])
        acc_scr[...] = acc
        return m_new, l

    def body(c2, carry):
        m, l = carry
        c = 2 * c2
        scores(c + 1, 1)
        m, l = update(c, 0, m, l)
        scores(c + 2, 0)
        return update(c + 1, 1, m, l)

    acc_scr[...] = jnp.zeros_like(acc_scr)
    scores(0, 0)
    m, l = lax.fori_loop(0, n_kv // 2 - 1, body,
                         (jnp.full((1, tq), -jnp.inf, F32), jnp.zeros((1, tq), F32)))
    scores(n_kv - 1, 1)
    m, l = update(n_kv - 2, 0, m, l)
    m, l = update(n_kv - 1, 1, m, l)
    o_ref[0] = (acc_scr[...] / l).astype(BF16)


def _attention(qt, k, vt, bn, seq, tq, tk):
    tv = vt.shape[2]
    t = k.shape[0]
    heads = C_HEADS
    nq = seq // tq
    kern = functools.partial(_attn_kernel, tk=tk)
    return pl.pallas_call(
        kern,
        grid=(bn, heads, nq),
        in_specs=[
            pl.BlockSpec((2 * LANES, tq), lambda b, h, i: (h, b * nq + i)),
            pl.BlockSpec((seq, 2 * LANES), lambda b, h, i: (b, h)),
            pl.BlockSpec((seq // tv, V_HEAD, tv), lambda b, h, i: (b, h, 0)),
        ],
        out_specs=pl.BlockSpec((1, V_HEAD, tq), lambda b, h, i: (b * nq + i, h, 0)),
        out_shape=jax.ShapeDtypeStruct((t // tq, heads * V_HEAD, tq), BF16),
        scratch_shapes=[pltpu.VMEM((2, tk, tq), F32), pltpu.VMEM((V_HEAD, tq), F32)],
        compiler_params=_cparams(("parallel", "parallel", "parallel"), 48),
        name="mla_attention",
    )(qt, k, vt)


ROW_ALIGN = 16
SLAB_SIZES = (512, 256, 128, 64, 32, 16)


def _slab_copies(src_ref, src_row, dst_ref, dst_row, n_rows, sem, fn):
    done = 0
    for size in SLAB_SIZES:
        take = (n_rows & size) != 0
        s0 = pl.multiple_of(src_row + done, ROW_ALIGN)
        d0 = pl.multiple_of(dst_row + done, ROW_ALIGN)

        @pl.when(take)
        def _(s0=s0, d0=d0, size=size):
            fn(pltpu.make_async_copy(src_ref.at[pl.ds(s0, size), :], dst_ref.at[pl.ds(d0, size), :], sem))

        done = done + jnp.where(take, size, 0)


def _attn_out_kernel(x_ref, mod_ref, o_ref, wo_ref, g_ref, wr_ref, x3_ref, h_ref, info_ref, tbl_ref,
                     run_scr, *, n_experts):
    @pl.when(pl.program_id(0) == 0)
    def _():
        run_scr[...] = jnp.zeros_like(run_scr)

    y = lax.dot_general(o_ref[0], wo_ref[...], _TN_DIMS, preferred_element_type=F32)
    x3 = x_ref[...] + mod_ref[0, 2:3, :] * y
    x3_ref[...] = x3
    h = _norm_mod(x3, g_ref[...], mod_ref[0, 3:4, :], mod_ref[0, 4:5, :])
    h_ref[...] = h.astype(BF16)
    tm, d = h.shape

    logits = jnp.dot(h, wr_ref[...], preferred_element_type=F32, precision=HIGHEST)
    lane = lax.broadcasted_iota(jnp.int32, (tm, LANES), 1).astype(F32)
    neg = jnp.float32(-jnp.inf)
    lg = jnp.where(lane < n_experts, logits, neg)
    m1 = jnp.max(lg, axis=-1, keepdims=True)
    i1 = jnp.min(jnp.where(lg == m1, lane, float(LANES)), axis=-1, keepdims=True)
    lg2 = jnp.where(lane == i1, neg, lg)
    m2 = jnp.max(lg2, axis=-1, keepdims=True)
    i2 = jnp.min(jnp.where(lg2 == m2, lane, float(LANES)), axis=-1, keepdims=True)
    e = jnp.exp(m2 - m1)
    g0 = 1.0 / (1.0 + e)
    g1 = e / (1.0 + e)
    sel1 = lane == i1
    sel2 = lane == i2
    onehot = jnp.where(sel1, 1.0, 0.0) + jnp.where(sel2, 1.0, 0.0)
    row = lax.broadcasted_iota(jnp.int32, (tm, tm), 0)
    col = lax.broadcasted_iota(jnp.int32, (tm, tm), 1)
    tri = jnp.where(col < row, 1.0, 0.0).astype(BF16)
    before = _dot(tri, onehot.astype(BF16))
    n_pick = jnp.sum(onehot, axis=0, keepdims=True)
    n_slab = jnp.floor((n_pick + (ROW_ALIGN - 1)) * (1.0 / ROW_ALIGN)) * ROW_ALIGN
    lrow = lax.broadcasted_iota(jnp.int32, (LANES, LANES), 0)
    lcol = lax.broadcasted_iota(jnp.int32, (LANES, LANES), 1)
    upper = jnp.where(lrow < lcol, 1.0, 0.0).astype(BF16)
    n8 = jnp.broadcast_to(n_slab, (SUBLANES, LANES))
    loc = _dot(n8.astype(BF16), upper)[0:1]
    where = before + loc
    pos0 = jnp.sum(jnp.where(sel1, where, 0.0), axis=-1, keepdims=True)
    pos1 = jnp.sum(jnp.where(sel2, where, 0.0), axis=-1, keepdims=True)
    info_ref[...] = jnp.where(lane == 0, pos0, jnp.where(lane == 1, pos1, jnp.where(
        lane == 2, g0, jnp.where(lane == 3, g1, 0.0))))

    off = run_scr[...]
    run_scr[...] = off + n_slab
    trow = lax.broadcasted_iota(jnp.int32, (SUBLANES, LANES), 0)
    tbl = jnp.where(trow == 0, n_slab, jnp.where(trow == 1, off, jnp.where(trow == 2, loc, 0.0)))
    tbl_ref[0] = tbl.astype(jnp.int32)


def _attn_out(x2, mod, o, w_o, norm_g, w_router, seq, tm):
    t, d = x2.shape
    n_experts = w_router.shape[1]
    wr = jnp.zeros((d, LANES), F32).at[:, :n_experts].set(w_router)
    kern = functools.partial(_attn_out_kernel, n_experts=n_experts)
    o_split = o.shape[2] // tm
    const = lambda i: (0, 0)
    return pl.pallas_call(
        kern,
        grid=(t // tm,),
        in_specs=[
            pl.BlockSpec((tm, d), lambda i: (i, 0)),
            pl.BlockSpec((1, 6, d), _row_block(tm, seq)),
            pl.BlockSpec((1, o.shape[1], tm), lambda i: (i // o_split, 0, i % o_split)),
            pl.BlockSpec(w_o.shape, const),
            pl.BlockSpec((1, d), const),
            pl.BlockSpec((d, LANES), const),
        ],
        out_specs=[
            pl.BlockSpec((tm, d), lambda i: (i, 0)),
            pl.BlockSpec((tm, d), lambda i: (i, 0)),
            pl.BlockSpec((tm, LANES), lambda i: (i, 0)),
            pl.BlockSpec((1, SUBLANES, LANES), lambda i: (i, 0, 0)),
        ],
        out_shape=[
            jax.ShapeDtypeStruct((t, d), F32),
            jax.ShapeDtypeStruct((t, d), BF16),
            jax.ShapeDtypeStruct((t, LANES), F32),
            jax.ShapeDtypeStruct((t // tm, SUBLANES, LANES), jnp.int32),
        ],
        scratch_shapes=[pltpu.VMEM((1, LANES), F32)],
        compiler_params=_cparams(("arbitrary",), 40),
        name="attn_out_router",
    )(x2, mod, o, w_o.astype(BF16), norm_g.reshape(1, d), wr)


def _dispatch_kernel(tbl_ref, h_ref, info_ref, xr_hbm, xs_scr, zero_scr, sem, *, n_experts, n_fill):
    i = pl.program_id(0)
    n_blk = pl.num_programs(0)
    tm = h_ref.shape[0]
    ns = xs_scr.shape[0]
    base = i * (3 * n_experts)
    starts = n_blk * (3 * n_experts)

    srow = lax.broadcasted_iota(jnp.int32, (tm, ns), 1).astype(F32)
    pick = jnp.logical_or(srow == info_ref[:, 0:1], srow == info_ref[:, 1:2])
    pick = jnp.where(pick, 1.0, 0.0).astype(BF16)
    xs_scr[...] = lax.dot_general(pick, h_ref[...], _TN_DIMS, preferred_element_type=F32).astype(BF16)

    def slabs(fn):
        for e in range(n_experts):
            dst = tbl_ref[starts + e] + tbl_ref[base + n_experts + e]
            _slab_copies(xs_scr, tbl_ref[base + 2 * n_experts + e], xr_hbm, dst, tbl_ref[base + e], sem, fn)

    slabs(lambda cp: cp.start())
    slabs(lambda cp: cp.wait())

    @pl.when(i == n_blk - 1)
    def _():
        zero_scr[...] = jnp.zeros_like(zero_scr)
        used = tbl_ref[starts + n_experts]

        def fills(fn):
            for e in range(n_experts):
                end = tbl_ref[starts + e] + tbl_ref[base + n_experts + e] + tbl_ref[base + e]
                _slab_copies(zero_scr, 0, xr_hbm, end, (-end) & (MOE_PAD - 1), sem, fn)
            for j in range(n_fill):
                row = pl.multiple_of(used + j * MOE_PAD, MOE_PAD)

                @pl.when(row < xr_hbm.shape[0])
                def _(row=row):
                    fn(pltpu.make_async_copy(zero_scr, xr_hbm.at[pl.ds(row, MOE_PAD), :], sem))

        fills(lambda cp: cp.start())
        fills(lambda cp: cp.wait())


def _dispatch(tbl, hb, info, tm, ns, n_rows, n_experts):
    t, d = hb.shape
    n_fill = n_rows // MOE_PAD - (TOP_K * t) // MOE_PAD
    kern = functools.partial(_dispatch_kernel, n_experts=n_experts, n_fill=n_fill)
    return pl.pallas_call(
        kern,
        grid_spec=pltpu.PrefetchScalarGridSpec(
            num_scalar_prefetch=1,
            grid=(t // tm,),
            in_specs=[
                pl.BlockSpec((tm, d), lambda i, tb: (i, 0)),
                pl.BlockSpec((tm, LANES), lambda i, tb: (i, 0)),
            ],
            out_specs=pl.BlockSpec(memory_space=pl.ANY),
            scratch_shapes=[pltpu.VMEM((ns, d), BF16), pltpu.VMEM((MOE_PAD, d), BF16),
                            pltpu.SemaphoreType.DMA(())],
        ),
        out_shape=jax.ShapeDtypeStruct((n_rows, d), BF16),
        compiler_params=_cparams(("arbitrary",), 40),
        name="moe_dispatch",
    )(tbl, hb, info)


def _experts_kernel(be_ref, na_ref, x_ref, w1_ref, w3_ref, w2_ref, o_ref, acc_scr):
    b = pl.program_id(0)
    f = pl.program_id(1)
    active = b < na_ref[0]
    last = f == pl.num_programs(1) - 1

    @pl.when(jnp.logical_and(active, f == 0))
    def _():
        acc_scr[...] = jnp.zeros_like(acc_scr)

    @pl.when(active)
    def _():
        x = x_ref[...]
        a = _dot(x, w1_ref[0])
        g = _dot(x, w3_ref[0])
        act = (a * jax.nn.sigmoid(a) * g).astype(BF16)
        acc_scr[...] += _dot(act, w2_ref[0])

    @pl.when(jnp.logical_and(active, last))
    def _():
        o_ref[...] = acc_scr[...].astype(BF16)

    @pl.when(jnp.logical_and(jnp.logical_not(active), last))
    def _():
        o_ref[...] = jnp.zeros_like(o_ref)


def _experts(blk_expert, n_active, xr, w1, w3, w2, tf):
    rows = MOE_PAD
    _, d, ff = w1.shape
    last_f = ff // tf - 1

    def ftile(b, f, na):
        return jnp.where(b < na[0], f, last_f)

    return pl.pallas_call(
        _experts_kernel,
        grid_spec=pltpu.PrefetchScalarGridSpec(
            num_scalar_prefetch=2,
            grid=(xr.shape[0] // rows, ff // tf),
            in_specs=[
                pl.BlockSpec((rows, d), lambda b, f, be, na: (b, 0)),
                pl.BlockSpec((1, d, tf), lambda b, f, be, na: (be[b], 0, ftile(b, f, na))),
                pl.BlockSpec((1, d, tf), lambda b, f, be, na: (be[b], 0, ftile(b, f, na))),
                pl.BlockSpec((1, tf, d), lambda b, f, be, na: (be[b], ftile(b, f, na), 0)),
            ],
            out_specs=pl.BlockSpec((rows, d), lambda b, f, be, na: (b, 0)),
            scratch_shapes=[pltpu.VMEM((rows, d), F32)],
        ),
        out_shape=jax.ShapeDtypeStruct(xr.shape, BF16),
        compiler_params=_cparams(("parallel", "arbitrary"), 48),
        name="moe_experts",
    )(blk_expert, n_active, xr, w1, w3, w2)


def _combine_kernel(tbl_ref, x_ref, mod_ref, info_ref, fg_ref, y_hbm, o_ref, ys_scr, sem, *, n_experts):
    tm, d = x_ref.shape
    ns = ys_scr.shape[0]
    base = pl.program_id(0) * (3 * n_experts)
    starts = pl.num_programs(0) * (3 * n_experts)
    ys_scr[...] = jnp.zeros_like(ys_scr)

    def slabs(fn):
        for e in range(n_experts):
            src = tbl_ref[starts + e] + tbl_ref[base + n_experts + e]
            _slab_copies(y_hbm, src, ys_scr, tbl_ref[base + 2 * n_experts + e], tbl_ref[base + e], sem, fn)

    slabs(lambda cp: cp.start())
    slabs(lambda cp: cp.wait())

    ys = ys_scr[...]
    srow = lax.broadcasted_iota(jnp.int32, (tm, ns), 1).astype(F32)
    y0 = _dot(jnp.where(srow == info_ref[:, 0:1], 1.0, 0.0).astype(BF16), ys)
    y1 = _dot(jnp.where(srow == info_ref[:, 1:2], 1.0, 0.0).astype(BF16), ys)
    y = info_ref[:, 2:3] * y0 + info_ref[:, 3:4] * y1
    x4 = x_ref[...] + mod_ref[0, 5:6, :] * y
    o_ref[...] = x4 * lax.rsqrt(jnp.mean(x4 * x4, axis=-1, keepdims=True) + RMS_EPS) * fg_ref[...]


def _combine(tbl, x3, mod, info, final_g, yr, seq, tm, ns, n_experts):
    t, d = x3.shape
    kern = functools.partial(_combine_kernel, n_experts=n_experts)
    return pl.pallas_call(
        kern,
        grid_spec=pltpu.PrefetchScalarGridSpec(
            num_scalar_prefetch=1,
            grid=(t // tm,),
            in_specs=[
                pl.BlockSpec((tm, d), lambda i, tb: (i, 0)),
                pl.BlockSpec((1, 6, d), _row_block(tm, seq)),
                pl.BlockSpec((tm, LANES), lambda i, tb: (i, 0)),
                pl.BlockSpec((1, d), lambda i, tb: (0, 0)),
                pl.BlockSpec(memory_space=pl.ANY),
            ],
            out_specs=pl.BlockSpec((tm, d), lambda i, tb: (i, 0)),
            scratch_shapes=[pltpu.VMEM((ns, d), BF16), pltpu.SemaphoreType.DMA(())],
        ),
        out_shape=jax.ShapeDtypeStruct((t, d), F32),
        compiler_params=_cparams(("arbitrary",), 40),
        name="moe_combine_final",
    )(tbl, x3, mod, info, final_g.reshape(1, d), yr)


def _moe(x3, mod, hb, info, tbl, w1, w3, w2, final_g, seq, tm):
    t, d = x3.shape
    n_experts = w1.shape[0]
    n_blk = t // tm
    ns = -(-(TOP_K * tm + n_experts * (ROW_ALIGN - 1)) // LANES) * LANES
    steps = (TOP_K * t + n_experts * n_blk * (ROW_ALIGN - 1)) // MOE_PAD + n_experts
    end = tbl[-1, 1, :n_experts] + tbl[-1, 0, :n_experts]
    nb = (end + MOE_PAD - 1) // MOE_PAD
    cum = jnp.cumsum(nb)
    n_active = cum[-1:].astype(jnp.int32)
    step = jnp.arange(steps, dtype=jnp.int32)
    owner = jnp.sum(jnp.minimum(step, n_active[0] - 1)[:, None] >= cum[None, :], axis=1)
    blk_expert = jnp.minimum(owner, n_experts - 1).astype(jnp.int32)
    flat = jnp.concatenate([tbl[:, :3, :n_experts].reshape(-1), (cum - nb) * MOE_PAD, n_active * MOE_PAD])
    flat = flat.astype(jnp.int32)
    xr = _dispatch(flat, hb, info, tm, ns, steps * MOE_PAD, n_experts)
    yr = _experts(blk_expert, n_active, xr, w1.astype(BF16), w3.astype(BF16), w2.astype(BF16),
                  w1.shape[2] // 2)
    return _combine(flat, x3, mod, info, final_g, yr, seq, tm, ns, n_experts)


def kernel(x, c, ada_w, ada_b, norm_mix_g, norm_ffn_g, final_g, ab_w_in, sgu_v_g, sgu_w_s, sgu_b_s, s5_lam_re, s5_lam_im, s5_log_dt, s5_b_re, s5_b_im, s5_c_re, s5_c_im, s5_d, s5_w_glu, ab_w_out, ffn_w1, ffn_w3, ffn_w2, mla_w_in, mla_q_norm_g, mla_kv_norm_g, mla_w_uq, mla_w_ukv, mla_w_o, moe_w_router, moe_w1, moe_w3, moe_w2):
    bn, seq, d = x.shape
    t = bn * seq
    tm = min(512, seq)
    x2 = x.reshape(t, d)
    mod = _adaln(c, ada_w, ada_b)

    a_out, u_tiles = _l0_in(x2, mod[0], norm_mix_g[0], ab_w_in[0], sgu_w_s[0], sgu_b_s[0], sgu_v_g[0], seq, tm)
    t_mat, ws, wy, aq = _s5_weights(s5_lam_re[0], s5_lam_im[0], s5_log_dt[0], s5_b_re[0], s5_b_im[0],
                                    s5_c_re[0], s5_c_im[0], S5_Q)
    y5 = _s5(u_tiles, t_mat, ws, wy, aq, bn, S5_Q)
    x2 = _l0_out(x2, mod[0], a_out, y5, u_tiles, s5_d[0], s5_w_glu[0], ab_w_out[0], seq, tm)
    x2 = _ffn(x2, mod[0], norm_ffn_g[0], ffn_w1[0], ffn_w3[0], ffn_w2[0], seq, tm, ffn_w1.shape[2] // 2)

    qt, k, vt = _mla_proj(x2, mod[1], norm_mix_g[1], mla_w_in[0], mla_q_norm_g[0], mla_kv_norm_g[0],
                          mla_w_uq[0], mla_w_ukv[0], seq, tm)
    o = _attention(qt, k, vt, bn, seq, min(1024, seq), min(1024, seq // 2))
    x3, hb, info, tbl = _attn_out(x2, mod[1], o, mla_w_o[0], norm_ffn_g[1], moe_w_router[0], seq, tm)
    out = _moe(x3, mod[1], hb, info, tbl, moe_w1[0], moe_w3[0], moe_w2[0], final_g, seq, tm)
    return out.reshape(bn, seq, d)
```

```python
import functools
import math

import jax
import jax.numpy as jnp
from jax import lax
from jax.experimental import pallas as pl
from jax.experimental.pallas import tpu as pltpu

F32 = jnp.float32
BF16 = jnp.bfloat16
HIGHEST = lax.Precision.HIGHEST

LANES = 128
SUBLANES = 8
MIB = 1024 * 1024

RMS_EPS = 1e-6
C_HEADS = 8
QK_NOPE = 128
QK_ROPE = 64
V_HEAD = 128
ROPE_THETA = 10000.0
TOP_K = 2

S5_Q = 8
MOE_PAD = 512


def _cparams(semantics, vmem_mib):
    return pltpu.CompilerParams(dimension_semantics=semantics, vmem_limit_bytes=vmem_mib * MIB)


def _dot(a, b):
    return jnp.dot(a, b, preferred_element_type=F32)


def _norm_mod(x, g, shift, scale):
    y = x * lax.rsqrt(jnp.mean(x * x, axis=-1, keepdims=True) + RMS_EPS)
    return (y * g) * (1.0 + scale) + shift


def _row_block(tm, seq):
    per_batch = seq // tm
    return lambda i, *_: (i // per_batch, 0, 0)


def _adaln_kernel(c_ref, w_ref, b_ref, o_ref):
    c = c_ref[...]
    ca = c * jax.nn.sigmoid(c)
    o_ref[0] = jnp.dot(ca, w_ref[0], preferred_element_type=F32, precision=HIGHEST) + b_ref[0]


def _adaln(c, ada_w, ada_b):
    depth, d, n = ada_w.shape
    bn = c.shape[0]
    tn = n // 4
    c8 = jnp.zeros((SUBLANES, d), F32).at[:bn].set(c)
    out = pl.pallas_call(
        _adaln_kernel,
        grid=(depth, n // tn),
        in_specs=[
            pl.BlockSpec((SUBLANES, d), lambda l, j: (0, 0)),
            pl.BlockSpec((1, d, tn), lambda l, j: (l, 0, j)),
            pl.BlockSpec((1, 1, tn), lambda l, j: (l, 0, j)),
        ],
        out_specs=pl.BlockSpec((1, SUBLANES, tn), lambda l, j: (l, 0, j)),
        out_shape=jax.ShapeDtypeStruct((depth, SUBLANES, n), F32),
        compiler_params=_cparams(("parallel", "parallel"), 32),
        name="adaln",
    )(c8, ada_w, ada_b.reshape(depth, 1, n))
    return out[:, :bn].reshape(depth, bn, 6, d)


def _l0_in_kernel(x_ref, mod_ref, g_ref, win_ref, ws_ref, bs_ref, vg_ref, a_ref, u_ref, *,
                  heads, chunk, n_tiles):
    x = x_ref[...]
    h = _norm_mod(x, g_ref[...], mod_ref[0, 0:1, :], mod_ref[0, 1:2, :]).astype(BF16)
    z = _dot(h, win_ref[...])
    tm = x.shape[0]
    a_width = heads * LANES
    for g in range(heads):
        u = jax.nn.gelu(z[:, g * LANES:(g + 1) * LANES])
        v = jax.nn.gelu(z[:, a_width + g * LANES:a_width + (g + 1) * LANES])
        vn = v * lax.rsqrt(jnp.mean(v * v, axis=-1, keepdims=True) + RMS_EPS) * vg_ref[g:g + 1, :]
        vnb = vn.astype(BF16)
        bias = bs_ref[:, g:g + 1]
        for ci in range(tm // chunk):
            rows = slice(ci * chunk, (ci + 1) * chunk)
            s = _dot(ws_ref[g], vnb[rows]) + bias
            a_ref[rows, g * LANES:(g + 1) * LANES] = (u[rows] * s).astype(BF16)
    for j in range(n_tiles):
        u_ref[j] = z[:, 2 * a_width + j * LANES:2 * a_width + (j + 1) * LANES].astype(BF16)


def _l0_in(x2, mod, norm_g, w_in, w_s, b_s, v_g, seq, tm):
    t, d = x2.shape
    heads, chunk, _ = w_s.shape
    a_width = heads * LANES
    b_width = w_in.shape[1] - 2 * a_width
    n_tiles = b_width // LANES
    kern = functools.partial(_l0_in_kernel, heads=heads, chunk=chunk, n_tiles=n_tiles)
    return pl.pallas_call(
        kern,
        grid=(t // tm,),
        in_specs=[
            pl.BlockSpec((tm, d), lambda i: (i, 0)),
            pl.BlockSpec((1, 6, d), _row_block(tm, seq)),
            pl.BlockSpec((1, d), lambda i: (0, 0)),
            pl.BlockSpec(w_in.shape, lambda i: (0, 0)),
            pl.BlockSpec(w_s.shape, lambda i: (0, 0, 0)),
            pl.BlockSpec((chunk, heads), lambda i: (0, 0)),
            pl.BlockSpec((heads, LANES), lambda i: (0, 0)),
        ],
        out_specs=[
            pl.BlockSpec((tm, a_width), lambda i: (i, 0)),
            pl.BlockSpec((n_tiles, tm, LANES), lambda i: (0, i, 0)),
        ],
        out_shape=[
            jax.ShapeDtypeStruct((t, a_width), BF16),
            jax.ShapeDtypeStruct((n_tiles, t, LANES), BF16),
        ],
        compiler_params=_cparams(("parallel",), 40),
        name="l0_in_sgu",
    )(x2, mod, norm_g.reshape(1, d), w_in.astype(BF16), w_s.astype(BF16), b_s.T, v_g)


def _s5_weights(lam_re, lam_im, log_dt, b_re, b_im, c_re, c_im, q):
    _, groups, p = lam_re.shape
    h = b_re.shape[-1]
    gpt = LANES // h
    nt = groups // gpt
    dt = jnp.exp(log_dt.astype(F32))[..., None]
    mag = jnp.exp(lam_re * dt)
    ang = lam_im * dt
    ar, ai = mag * jnp.cos(ang), mag * jnp.sin(ang)
    den = lam_re * lam_re + lam_im * lam_im
    fr = ((ar - 1.0) * lam_re + ai * lam_im) / den
    fi = (ai * lam_re - (ar - 1.0) * lam_im) / den
    bbr = fr[..., None] * b_re - fi[..., None] * b_im
    bbi = fr[..., None] * b_im + fi[..., None] * b_re
    prs, pis = [jnp.ones_like(ar)], [jnp.zeros_like(ar)]
    for _ in range(q):
        prs.append(prs[-1] * ar - pis[-1] * ai)
        pis.append(prs[-2] * ai + pis[-1] * ar)
    pr, pi = jnp.stack(prs), jnp.stack(pis)
    abr = pr[..., None] * bbr - pi[..., None] * bbi
    abi = pr[..., None] * bbi + pi[..., None] * bbr
    car = c_re * pr[:, :, :, None, :] - c_im * pi[:, :, :, None, :]
    cai = c_re * pi[:, :, :, None, :] + c_im * pr[:, :, :, None, :]
    kk = (jnp.einsum('dgop,kdgpi->kdgoi', c_re, abr[:q], precision=HIGHEST)
          - jnp.einsum('dgop,kdgpi->kdgoi', c_im, abi[:q], precision=HIGHEST))
    idx = jnp.arange(q)
    dfi = idx[None, :] - idx[:, None]
    kf = jnp.where((dfi >= 0)[:, :, None, None, None], kk[:, 0][jnp.clip(dfi, 0, q - 1)], 0.0)
    kb = jnp.where((dfi <= 0)[:, :, None, None, None], kk[:, 1][jnp.clip(-dfi, 0, q - 1)], 0.0)
    m = (kf + kb).reshape(q, q, nt, gpt, h, h)
    w4 = jnp.stack([jnp.stack([abr[q - 1 - idx, 0], abi[q - 1 - idx, 0]]),
                    jnp.stack([abr[idx, 1], abi[idx, 1]])])
    w4 = w4.reshape(2, 2, q, nt, gpt, p, h)
    y4 = jnp.stack([jnp.stack([car[idx + 1, 0], -cai[idx + 1, 0]]),
                    jnp.stack([car[q - idx, 1], -cai[q - idx, 1]])])
    y4 = y4.reshape(2, 2, q, nt, gpt, h, p)

    a_in = m.transpose(2, 0, 3, 5, 1, 4).reshape(nt, q * LANES, q * h)
    a_ws = w4.transpose(3, 2, 4, 6, 0, 1, 5).reshape(nt, q * LANES, 4 * p)
    a_wy = y4.transpose(3, 0, 1, 4, 6, 2, 5).reshape(nt, 4 * gpt * p, q * h)

    def expand(a, col_src, row_group, col_group):
        pick = (jnp.arange(a.shape[2])[:, None] == col_src[None, :]).astype(BF16)
        wide = jnp.einsum('jrk,kc->jrc', a.astype(BF16), pick, preferred_element_type=F32)
        return jnp.where(row_group[:, None] == col_group[None, :], wide, 0.0).astype(BF16)

    r_in = jnp.arange(q * LANES)
    r_st = jnp.arange(4 * gpt * p)
    g_in = (r_in // h) % gpt
    g_st = (r_st // p) % gpt
    src_in = (r_in // LANES) * h + r_in % h
    src_st = (r_st // (gpt * p)) * p + r_st % p
    t_mat = expand(a_in, src_in, g_in, g_in)
    ws = expand(a_ws, src_st, g_in, g_st)
    wy = expand(a_wy, src_in, g_st, g_in)

    aq = jnp.stack([jnp.stack([pr[q, 0], pi[q, 0]]), jnp.stack([pr[q, 1], pi[q, 1]])])
    aq = aq.reshape(2, 2, nt, gpt, p).transpose(2, 0, 1, 3, 4).reshape(nt, 1, 4 * gpt * p)
    return t_mat, ws, wy, aq


def _s5_kernel(u_ref, t_ref, ws_ref, wy_ref, aq_ref, y_ref, s_ref):
    u = u_ref[0]
    s_ref[...] = _dot(u, ws_ref[0])
    rows, width = s_ref.shape
    hw = width // 4
    aq = aq_ref[0]
    afr, afi = aq[:, 0:hw], aq[:, hw:2 * hw]
    abr, abi = aq[:, 2 * hw:3 * hw], aq[:, 3 * hw:4 * hw]

    def body(c, carry):
        xfr, xfi, xbr, xbi = carry
        rf = pl.ds(c, 1)
        rb = pl.ds(rows - 1 - c, 1)
        sfr = s_ref[rf, 0:hw]
        sfi = s_ref[rf, hw:2 * hw]
        sbr = s_ref[rb, 2 * hw:3 * hw]
        sbi = s_ref[rb, 3 * hw:4 * hw]
        s_ref[rf, 0:hw] = xfr
        s_ref[rf, hw:2 * hw] = xfi
        s_ref[rb, 2 * hw:3 * hw] = xbr
        s_ref[rb, 3 * hw:4 * hw] = xbi
        return (afr * xfr - afi * xfi + sfr, afr * xfi + afi * xfr + sfi,
                abr * xbr - abi * xbi + sbr, abr * xbi + abi * xbr + sbi)

    zero = jnp.zeros((1, hw), F32)
    lax.fori_loop(0, rows, body, (zero, zero, zero, zero), unroll=8)
    y_ref[0] = _dot(u, t_ref[0]) + _dot(s_ref[...].astype(BF16), wy_ref[0])


def _s5(u_tiles, t_mat, ws, wy, aq, bn, q):
    nt, t, _ = u_tiles.shape
    rows = t // q // bn
    u2 = u_tiles.reshape(nt, t // q, q * LANES)
    sw = ws.shape[2]
    y2 = pl.pallas_call(
        _s5_kernel,
        grid=(nt, bn),
        in_specs=[
            pl.BlockSpec((1, rows, q * LANES), lambda j, b: (j, b, 0)),
            pl.BlockSpec((1,) + t_mat.shape[1:], lambda j, b: (j, 0, 0)),
            pl.BlockSpec((1,) + ws.shape[1:], lambda j, b: (j, 0, 0)),
            pl.BlockSpec((1,) + wy.shape[1:], lambda j, b: (j, 0, 0)),
            pl.BlockSpec((1, 1, sw), lambda j, b: (j, 0, 0)),
        ],
        out_specs=pl.BlockSpec((1, rows, q * LANES), lambda j, b: (j, b, 0)),
        out_shape=jax.ShapeDtypeStruct((nt, t // q, q * LANES), F32),
        scratch_shapes=[pltpu.VMEM((rows, sw), F32)],
        compiler_params=_cparams(("parallel", "parallel"), 48),
        name="s5_chunked",
    )(u2, t_mat, ws, wy, aq)
    return y2.reshape(nt, t, LANES)


def _l0_out_kernel(x_ref, mod_ref, a_ref, y5_ref, u_ref, d_ref, wglu_ref, wout_ref, o_ref, *, n_tiles):
    ys = []
    for j in range(n_tiles):
        ys.append(jax.nn.gelu(y5_ref[j] + d_ref[j] * u_ref[j].astype(F32)))
    y = jnp.concatenate(ys, axis=-1)
    glu = jax.nn.sigmoid(_dot(y.astype(BF16), wglu_ref[...]))
    b_out = (y * glu).astype(BF16)
    a_width = a_ref.shape[1]
    out = _dot(a_ref[...], wout_ref[0:a_width, :]) + _dot(b_out, wout_ref[a_width:, :])
    o_ref[...] = x_ref[...] + mod_ref[0, 2:3, :] * out


def _l0_out(x2, mod, a_out, y5, u_tiles, d_skip, w_glu, w_out, seq, tm):
    t, d = x2.shape
    n_tiles = u_tiles.shape[0]
    a_width = a_out.shape[1]
    kern = functools.partial(_l0_out_kernel, n_tiles=n_tiles)
    return pl.pallas_call(
        kern,
        grid=(t // tm,),
        in_specs=[
            pl.BlockSpec((tm, d), lambda i: (i, 0)),
            pl.BlockSpec((1, 6, d), _row_block(tm, seq)),
            pl.BlockSpec((tm, a_width), lambda i: (i, 0)),
            pl.BlockSpec((n_tiles, tm, LANES), lambda i: (0, i, 0)),
            pl.BlockSpec((n_tiles, tm, LANES), lambda i: (0, i, 0)),
            pl.BlockSpec((n_tiles, 1, LANES), lambda i: (0, 0, 0)),
            pl.BlockSpec(w_glu.shape, lambda i: (0, 0)),
            pl.BlockSpec(w_out.shape, lambda i: (0, 0)),
        ],
        out_specs=pl.BlockSpec((tm, d), lambda i: (i, 0)),
        out_shape=jax.ShapeDtypeStruct((t, d), F32),
        compiler_params=_cparams(("parallel",), 40),
        name="l0_out",
    )(x2, mod, a_out, y5, u_tiles, d_skip.reshape(n_tiles, 1, LANES),
      w_glu.astype(BF16), w_out.astype(BF16))


def _ffn_kernel(x_ref, mod_ref, g_ref, w1_ref, w3_ref, w2_ref, o_ref, h_scr, acc_scr):
    f = pl.program_id(1)

    @pl.when(f == 0)
    def _():
        h_scr[...] = _norm_mod(x_ref[...], g_ref[...], mod_ref[0, 3:4, :], mod_ref[0, 4:5, :]).astype(BF16)
        acc_scr[...] = jnp.zeros_like(acc_scr)

    h = h_scr[...]
    a = _dot(h, w1_ref[...])
    b = _dot(h, w3_ref[...])
    act = (a * jax.nn.sigmoid(a) * b).astype(BF16)
    acc_scr[...] += _dot(act, w2_ref[...])

    @pl.when(f == pl.num_programs(1) - 1)
    def _():
        o_ref[...] = x_ref[...] + mod_ref[0, 5:6, :] * acc_scr[...]


def _ffn(x2, mod, norm_g, w1, w3, w2, seq, tm, tf):
    t, d = x2.shape
    ff = w1.shape[1]
    return pl.pallas_call(
        _ffn_kernel,
        grid=(t // tm, ff // tf),
        in_specs=[
            pl.BlockSpec((tm, d), lambda i, f: (i, 0)),
            pl.BlockSpec((1, 6, d), _row_block(tm, seq)),
            pl.BlockSpec((1, d), lambda i, f: (0, 0)),
            pl.BlockSpec((d, tf), lambda i, f: (0, f)),
            pl.BlockSpec((d, tf), lambda i, f: (0, f)),
            pl.BlockSpec((tf, d), lambda i, f: (f, 0)),
        ],
        out_specs=pl.BlockSpec((tm, d), lambda i, f: (i, 0)),
        out_shape=jax.ShapeDtypeStruct((t, d), F32),
        scratch_shapes=[pltpu.VMEM((tm, d), BF16), pltpu.VMEM((tm, d), F32)],
        compiler_params=_cparams(("parallel", "arbitrary"), 48),
        name="ffn_swiglu",
    )(x2, mod, norm_g.reshape(1, d), w1.astype(BF16), w3.astype(BF16), w2.astype(BF16))


_NT_DIMS = (((1,), (1,)), ((), ()))
_TN_DIMS = (((0,), (0,)), ((), ()))


def _mla_proj_kernel(x_ref, mod_ref, g_ref, win_ref, gq_ref, gkv_ref, wqt_ref, wkn_ref, wvt_ref,
                     cos_ref, sin_ref, cost_ref, sint_ref, qt_ref, k_ref, vt_ref, *, q_lora, kv_lora, heads):
    h = _norm_mod(x_ref[...], g_ref[...], mod_ref[0, 0:1, :], mod_ref[0, 1:2, :]).astype(BF16)
    z = _dot(h, win_ref[...])
    cq = z[:, :q_lora]
    cq = (cq * lax.rsqrt(jnp.mean(cq * cq, axis=-1, keepdims=True) + RMS_EPS) * gq_ref[...]).astype(BF16)
    ckv = z[:, q_lora:q_lora + kv_lora]
    ckv = (ckv * lax.rsqrt(jnp.mean(ckv * ckv, axis=-1, keepdims=True) + RMS_EPS) * gkv_ref[...]).astype(BF16)
    r0 = q_lora + kv_lora
    k_rope = (z[:, r0:r0 + LANES] * cos_ref[...] + z[:, r0 + LANES:r0 + 2 * LANES] * sin_ref[...]).astype(BF16)
    kn = _dot(ckv, wkn_ref[...])
    qat = lax.dot_general(wqt_ref[...], cq, _NT_DIMS, preferred_element_type=F32)
    vt_ref[0] = lax.dot_general(wvt_ref[...], ckv, _NT_DIMS, preferred_element_type=F32).astype(BF16)
    cost = cost_ref[...]
    sint = sint_ref[...]
    hw = heads * LANES
    for hd in range(heads):
        c = slice(hd * LANES, (hd + 1) * LANES)
        qt_ref[2 * hd * LANES:(2 * hd + 1) * LANES, :] = qat[c, :].astype(BF16)
        q_rope = qat[hw + hd * LANES:hw + (hd + 1) * LANES, :] * cost \
            + qat[2 * hw + hd * LANES:2 * hw + (hd + 1) * LANES, :] * sint
        qt_ref[(2 * hd + 1) * LANES:(2 * hd + 2) * LANES, :] = q_rope.astype(BF16)
        k_ref[:, 2 * hd * LANES:(2 * hd + 1) * LANES] = kn[:, c].astype(BF16)
        k_ref[:, (2 * hd + 1) * LANES:(2 * hd + 2) * LANES] = k_rope


def _rope_pad(w_rope):
    half = QK_ROPE // 2
    x1, x2 = w_rope[..., :half], w_rope[..., half:]
    zeros = jnp.zeros(w_rope.shape[:-1] + (LANES - QK_ROPE,), w_rope.dtype)
    return jnp.concatenate([x1, x2, zeros], -1), jnp.concatenate([-x2, x1, zeros], -1)


def _mla_proj(x2, mod, norm_g, w_in, gq, gkv, w_uq, w_ukv, seq, tm):
    t, d = x2.shape
    q_lora, kv_lora = gq.shape[0], gkv.shape[0]
    heads = C_HEADS
    scale = math.log2(math.e) / math.sqrt(QK_NOPE + QK_ROPE)
    kr, krs = _rope_pad(w_in[:, q_lora + kv_lora:])
    w_in_ext = jnp.concatenate([w_in[:, :q_lora + kv_lora], kr, krs], -1).astype(BF16)
    wq3 = (w_uq * scale).reshape(q_lora, heads, QK_NOPE + QK_ROPE)
    qr, qrs = _rope_pad(wq3[..., QK_NOPE:])
    wqt = jnp.concatenate([wq3[..., :QK_NOPE].reshape(q_lora, -1), qr.reshape(q_lora, -1),
                           qrs.reshape(q_lora, -1)], -1).T.astype(BF16)
    wkv3 = w_ukv.reshape(kv_lora, heads, QK_NOPE + V_HEAD)
    wkn = wkv3[..., :QK_NOPE].reshape(kv_lora, -1).astype(BF16)
    wvt = wkv3[..., QK_NOPE:].reshape(kv_lora, -1).T.astype(BF16)
    inv_freq = ROPE_THETA ** (-jnp.arange(0, QK_ROPE, 2, dtype=F32) / QK_ROPE)
    ang = jnp.arange(seq, dtype=F32)[:, None] * inv_freq[None, :]
    pad = jnp.zeros((seq, LANES - QK_ROPE), F32)
    cos = jnp.concatenate([jnp.cos(ang), jnp.cos(ang), pad], -1)
    sin = jnp.concatenate([jnp.sin(ang), jnp.sin(ang), pad], -1)
    per_batch = seq // tm
    kern = functools.partial(_mla_proj_kernel, q_lora=q_lora, kv_lora=kv_lora, heads=heads)
    const = lambda i: (0, 0)
    return pl.pallas_call(
        kern,
        grid=(t // tm,),
        in_specs=[
            pl.BlockSpec((tm, d), lambda i: (i, 0)),
            pl.BlockSpec((1, 6, d), _row_block(tm, seq)),
            pl.BlockSpec((1, d), const),
            pl.BlockSpec(w_in_ext.shape, const),
            pl.BlockSpec((1, q_lora), const),
            pl.BlockSpec((1, kv_lora), const),
            pl.BlockSpec(wqt.shape, const),
            pl.BlockSpec(wkn.shape, const),
            pl.BlockSpec(wvt.shape, const),
            pl.BlockSpec((tm, LANES), lambda i: (i % per_batch, 0)),
            pl.BlockSpec((tm, LANES), lambda i: (i % per_batch, 0)),
            pl.BlockSpec((LANES, tm), lambda i: (0, i % per_batch)),
            pl.BlockSpec((LANES, tm), lambda i: (0, i % per_batch)),
        ],
        out_specs=[
            pl.BlockSpec((2 * heads * LANES, tm), lambda i: (0, i)),
            pl.BlockSpec((tm, 2 * heads * LANES), lambda i: (i, 0)),
            pl.BlockSpec((1, heads * V_HEAD, tm), lambda i: (i, 0, 0)),
        ],
        out_shape=[
            jax.ShapeDtypeStruct((2 * heads * LANES, t), BF16),
            jax.ShapeDtypeStruct((t, 2 * heads * LANES), BF16),
            jax.ShapeDtypeStruct((t // tm, heads * V_HEAD, tm), BF16),
        ],
        compiler_params=_cparams(("parallel",), 48),
        name="mla_proj",
    )(x2, mod, norm_g.reshape(1, d), w_in_ext, gq.reshape(1, -1), gkv.reshape(1, -1), wqt, wkn, wvt,
      cos, sin, cos.T, sin.T)


def _attn_kernel(qt_ref, k_ref, vt_ref, o_ref, s_scr, acc_scr, *, tk):
    tq = qt_ref.shape[1]
    n_kv = k_ref.shape[0] // tk
    tv = vt_ref.shape[2]
    sub = tk // tv

    def scores(c, slot):
        k = k_ref[pl.ds(pl.multiple_of(c * tk, tk), tk), :]
        s_scr[slot] = _dot(k, qt_ref[...])

    def update(c, slot, m, l):
        s = s_scr[slot]
        m_new = jnp.maximum(m, jnp.max(s, axis=0, keepdims=True))
        alpha = jnp.exp2(m - m_new)
        p = jnp.exp2(s - m_new)
        l = alpha * l + jnp.sum(p, axis=0, keepdims=True)
        pb = p.astype(BF16)
        acc = alpha * acc_scr[...]
        for j in range(sub):
            acc = acc + _dot(vt_ref[c * sub + j], pb[j * tv:(j + 1) * tv])
        acc_scr[...] = acc
        return m_new, l

    def body(c2, carry):
        m, l = carry
        c = 2 * c2
        scores(c + 1, 1)
        m, l = update(c, 0, m, l)
        scores(c + 2, 0)
        return update(c + 1, 1, m, l)

    acc_scr[...] = jnp.zeros_like(acc_scr)
    scores(0, 0)
    m, l = lax.fori_loop(0, n_kv // 2 - 1, body,
                         (jnp.full((1, tq), -jnp.inf, F32), jnp.zeros((1, tq), F32)))
    scores(n_kv - 1, 1)
    m, l = update(n_kv - 2, 0, m, l)
    m, l = update(n_kv - 1, 1, m, l)
    o_ref[0] = (acc_scr[...] / l).astype(BF16)


def _attention(qt, k, vt, bn, seq, tq, tk):
    tv = vt.shape[2]
    t = k.shape[0]
    heads = C_HEADS
    nq = seq // tq
    kern = functools.partial(_attn_kernel, tk=tk)
    return pl.pallas_call(
        kern,
        grid=(bn, heads, nq),
        in_specs=[
            pl.BlockSpec((2 * LANES, tq), lambda b, h, i: (h, b * nq + i)),
            pl.BlockSpec((seq, 2 * LANES), lambda b, h, i: (b, h)),
            pl.BlockSpec((seq // tv, V_HEAD, tv), lambda b, h, i: (b, h, 0)),
        ],
        out_specs=pl.BlockSpec((1, V_HEAD, tq), lambda b, h, i: (b * nq + i, h, 0)),
        out_shape=jax.ShapeDtypeStruct((t // tq, heads * V_HEAD, tq), BF16),
        scratch_shapes=[pltpu.VMEM((2, tk, tq), F32), pltpu.VMEM((V_HEAD, tq), F32)],
        compiler_params=_cparams(("parallel", "parallel", "parallel"), 48),
        name="mla_attention",
    )(qt, k, vt)


ROW_ALIGN = 16
SLAB_SIZES = (512, 256, 128, 64, 32, 16)


def _slab_copies(src_ref, src_row, dst_ref, dst_row, n_rows, sem, fn):
    done = 0
    for size in SLAB_SIZES:
        take = (n_rows & size) != 0
        s0 = pl.multiple_of(src_row + done, ROW_ALIGN)
        d0 = pl.multiple_of(dst_row + done, ROW_ALIGN)

        @pl.when(take)
        def _(s0=s0, d0=d0, size=size):
            fn(pltpu.make_async_copy(src_ref.at[pl.ds(s0, size), :], dst_ref.at[pl.ds(d0, size), :], sem))

        done = done + jnp.where(take, size, 0)


def _attn_out_kernel(x_ref, mod_ref, o_ref, wo_ref, g_ref, wr_ref, x3_ref, h_ref, info_ref, tbl_ref,
                     run_scr, *, n_experts):
    @pl.when(pl.program_id(0) == 0)
    def _():
        run_scr[...] = jnp.zeros_like(run_scr)

    y = lax.dot_general(o_ref[0], wo_ref[...], _TN_DIMS, preferred_element_type=F32)
    x3 = x_ref[...] + mod_ref[0, 2:3, :] * y
    x3_ref[...] = x3
    h = _norm_mod(x3, g_ref[...], mod_ref[0, 3:4, :], mod_ref[0, 4:5, :])
    h_ref[...] = h.astype(BF16)
    tm, d = h.shape

    h_hi = h.astype(BF16)
    h_lo = (h - h_hi.astype(F32)).astype(BF16)
    wr = wr_ref[...]
    w_hi = wr.astype(BF16)
    w_lo = (wr - w_hi.astype(F32)).astype(BF16)
    logits = _dot(h_hi, w_hi) + (_dot(h_hi, w_lo) + _dot(h_lo, w_hi))
    lane = lax.broadcasted_iota(jnp.int32, (tm, LANES), 1).astype(F32)
    neg = jnp.float32(-jnp.inf)
    lg = jnp.where(lane < n_experts, logits, neg)
    m1 = jnp.max(lg, axis=-1, keepdims=True)
    i1 = jnp.min(jnp.where(lg == m1, lane, float(LANES)), axis=-1, keepdims=True)
    lg2 = jnp.where(lane == i1, neg, lg)
    m2 = jnp.max(lg2, axis=-1, keepdims=True)
    i2 = jnp.min(jnp.where(lg2 == m2, lane, float(LANES)), axis=-1, keepdims=True)
    e = jnp.exp(m2 - m1)
    g0 = 1.0 / (1.0 + e)
    g1 = e / (1.0 + e)
    sel1 = lane == i1
    sel2 = lane == i2
    onehot = jnp.where(sel1, 1.0, 0.0) + jnp.where(sel2, 1.0, 0.0)
    row = lax.broadcasted_iota(jnp.int32, (tm, tm), 0)
    col = lax.broadcasted_iota(jnp.int32, (tm, tm), 1)
    tri = jnp.where(col < row, 1.0, 0.0).astype(BF16)
    before = _dot(tri, onehot.astype(BF16))
    n_pick = jnp.sum(onehot, axis=0, keepdims=True)
    n_slab = jnp.floor((n_pick + (ROW_ALIGN - 1)) * (1.0 / ROW_ALIGN)) * ROW_ALIGN
    lrow = lax.broadcasted_iota(jnp.int32, (LANES, LANES), 0)
    lcol = lax.broadcasted_iota(jnp.int32, (LANES, LANES), 1)
    upper = jnp.where(lrow < lcol, 1.0, 0.0).astype(BF16)
    n8 = jnp.broadcast_to(n_slab, (SUBLANES, LANES))
    loc = _dot(n8.astype(BF16), upper)[0:1]
    where = before + loc
    pos0 = jnp.sum(jnp.where(sel1, where, 0.0), axis=-1, keepdims=True)
    pos1 = jnp.sum(jnp.where(sel2, where, 0.0), axis=-1, keepdims=True)
    info_ref[...] = jnp.where(lane == 0, pos0, jnp.where(lane == 1, pos1, jnp.where(
        lane == 2, g0, jnp.where(lane == 3, g1, 0.0))))

    off = run_scr[...]
    run_scr[...] = off + n_slab
    trow = lax.broadcasted_iota(jnp.int32, (SUBLANES, LANES), 0)
    tbl = jnp.where(trow == 0, n_slab, jnp.where(trow == 1, off, jnp.where(trow == 2, loc, 0.0)))
    tbl_ref[0] = tbl.astype(jnp.int32)


def _attn_out(x2, mod, o, w_o, norm_g, w_router, seq, tm):
    t, d = x2.shape
    n_experts = w_router.shape[1]
    wr = jnp.zeros((d, LANES), F32).at[:, :n_experts].set(w_router)
    kern = functools.partial(_attn_out_kernel, n_experts=n_experts)
    o_split = o.shape[2] // tm
    const = lambda i: (0, 0)
    return pl.pallas_call(
        kern,
        grid=(t // tm,),
        in_specs=[
            pl.BlockSpec((tm, d), lambda i: (i, 0)),
            pl.BlockSpec((1, 6, d), _row_block(tm, seq)),
            pl.BlockSpec((1, o.shape[1], tm), lambda i: (i // o_split, 0, i % o_split)),
            pl.BlockSpec(w_o.shape, const),
            pl.BlockSpec((1, d), const),
            pl.BlockSpec((d, LANES), const),
        ],
        out_specs=[
            pl.BlockSpec((tm, d), lambda i: (i, 0)),
            pl.BlockSpec((tm, d), lambda i: (i, 0)),
            pl.BlockSpec((tm, LANES), lambda i: (i, 0)),
            pl.BlockSpec((1, SUBLANES, LANES), lambda i: (i, 0, 0)),
        ],
        out_shape=[
            jax.ShapeDtypeStruct((t, d), F32),
            jax.ShapeDtypeStruct((t, d), BF16),
            jax.ShapeDtypeStruct((t, LANES), F32),
            jax.ShapeDtypeStruct((t // tm, SUBLANES, LANES), jnp.int32),
        ],
        scratch_shapes=[pltpu.VMEM((1, LANES), F32)],
        compiler_params=_cparams(("arbitrary",), 40),
        name="attn_out_router",
    )(x2, mod, o, w_o.astype(BF16), norm_g.reshape(1, d), wr)


def _dispatch_kernel(tbl_ref, h_ref, info_ref, xr_hbm, xs_scr, zero_scr, sem, *, n_experts, n_fill):
    i = pl.program_id(0)
    n_blk = pl.num_programs(0)
    tm = h_ref.shape[0]
    ns = xs_scr.shape[0]
    base = i * (3 * n_experts)
    starts = n_blk * (3 * n_experts)

    srow = lax.broadcasted_iota(jnp.int32, (tm, ns), 1).astype(F32)
    pick = jnp.logical_or(srow == info_ref[:, 0:1], srow == info_ref[:, 1:2])
    pick = jnp.where(pick, 1.0, 0.0).astype(BF16)
    xs_scr[...] = lax.dot_general(pick, h_ref[...], _TN_DIMS, preferred_element_type=F32).astype(BF16)

    def slabs(fn):
        for e in range(n_experts):
            dst = tbl_ref[starts + e] + tbl_ref[base + n_experts + e]
            _slab_copies(xs_scr, tbl_ref[base + 2 * n_experts + e], xr_hbm, dst, tbl_ref[base + e], sem, fn)

    slabs(lambda cp: cp.start())
    slabs(lambda cp: cp.wait())

    @pl.when(i == n_blk - 1)
    def _():
        zero_scr[...] = jnp.zeros_like(zero_scr)
        used = tbl_ref[starts + n_experts]

        def fills(fn):
            for e in range(n_experts):
                end = tbl_ref[starts + e] + tbl_ref[base + n_experts + e] + tbl_ref[base + e]
                _slab_copies(zero_scr, 0, xr_hbm, end, (-end) & (MOE_PAD - 1), sem, fn)
            for j in range(n_fill):
                row = pl.multiple_of(used + j * MOE_PAD, MOE_PAD)

                @pl.when(row < xr_hbm.shape[0])
                def _(row=row):
                    fn(pltpu.make_async_copy(zero_scr, xr_hbm.at[pl.ds(row, MOE_PAD), :], sem))

        fills(lambda cp: cp.start())
        fills(lambda cp: cp.wait())


def _dispatch(tbl, hb, info, tm, ns, n_rows, n_experts):
    t, d = hb.shape
    n_fill = n_rows // MOE_PAD - (TOP_K * t) // MOE_PAD
    kern = functools.partial(_dispatch_kernel, n_experts=n_experts, n_fill=n_fill)
    return pl.pallas_call(
        kern,
        grid_spec=pltpu.PrefetchScalarGridSpec(
            num_scalar_prefetch=1,
            grid=(t // tm,),
            in_specs=[
                pl.BlockSpec((tm, d), lambda i, tb: (i, 0)),
                pl.BlockSpec((tm, LANES), lambda i, tb: (i, 0)),
            ],
            out_specs=pl.BlockSpec(memory_space=pl.ANY),
            scratch_shapes=[pltpu.VMEM((ns, d), BF16), pltpu.VMEM((MOE_PAD, d), BF16),
                            pltpu.SemaphoreType.DMA(())],
        ),
        out_shape=jax.ShapeDtypeStruct((n_rows, d), BF16),
        compiler_params=_cparams(("arbitrary",), 40),
        name="moe_dispatch",
    )(tbl, hb, info)


def _experts_kernel(be_ref, na_ref, x_ref, w1_ref, w3_ref, w2_ref, o_ref, acc_scr):
    b = pl.program_id(0)
    f = pl.program_id(1)
    active = b < na_ref[0]
    last = f == pl.num_programs(1) - 1

    @pl.when(jnp.logical_and(active, f == 0))
    def _():
        acc_scr[...] = jnp.zeros_like(acc_scr)

    @pl.when(active)
    def _():
        x = x_ref[...]
        a = _dot(x, w1_ref[0])
        g = _dot(x, w3_ref[0])
        act = (a * jax.nn.sigmoid(a) * g).astype(BF16)
        acc_scr[...] += _dot(act, w2_ref[0])

    @pl.when(jnp.logical_and(active, last))
    def _():
        o_ref[...] = acc_scr[...].astype(BF16)

    @pl.when(jnp.logical_and(jnp.logical_not(active), last))
    def _():
        o_ref[...] = jnp.zeros_like(o_ref)


def _experts(blk_expert, n_active, xr, w1, w3, w2, tf):
    rows = MOE_PAD
    _, d, ff = w1.shape
    last_f = ff // tf - 1

    def ftile(b, f, na):
        return jnp.where(b < na[0], f, last_f)

    return pl.pallas_call(
        _experts_kernel,
        grid_spec=pltpu.PrefetchScalarGridSpec(
            num_scalar_prefetch=2,
            grid=(xr.shape[0] // rows, ff // tf),
            in_specs=[
                pl.BlockSpec((rows, d), lambda b, f, be, na: (b, 0)),
                pl.BlockSpec((1, d, tf), lambda b, f, be, na: (be[b], 0, ftile(b, f, na))),
                pl.BlockSpec((1, d, tf), lambda b, f, be, na: (be[b], 0, ftile(b, f, na))),
                pl.BlockSpec((1, tf, d), lambda b, f, be, na: (be[b], ftile(b, f, na), 0)),
            ],
            out_specs=pl.BlockSpec((rows, d), lambda b, f, be, na: (b, 0)),
            scratch_shapes=[pltpu.VMEM((rows, d), F32)],
        ),
        out_shape=jax.ShapeDtypeStruct(xr.shape, BF16),
        compiler_params=_cparams(("parallel", "arbitrary"), 48),
        name="moe_experts",
    )(blk_expert, n_active, xr, w1, w3, w2)


def _combine_kernel(tbl_ref, x_ref, mod_ref, info_ref, fg_ref, y_hbm, o_ref, ys_scr, sem, *, n_experts):
    tm, d = x_ref.shape
    ns = ys_scr.shape[1]
    i = pl.program_id(0)
    n_blk = pl.num_programs(0)
    starts = n_blk * (3 * n_experts)

    def slabs(blk, fn):
        slot = blk % 2
        base = blk * (3 * n_experts)
        for e in range(n_experts):
            src = tbl_ref[starts + e] + tbl_ref[base + n_experts + e]
            _slab_copies(y_hbm, src, ys_scr.at[slot], tbl_ref[base + 2 * n_experts + e], tbl_ref[base + e],
                         sem.at[slot], fn)

    def fetch(blk):
        ys_scr[blk % 2] = jnp.zeros((ns, d), BF16)
        slabs(blk, lambda cp: cp.start())

    @pl.when(i == 0)
    def _():
        fetch(i)

    @pl.when(i + 1 < n_blk)
    def _():
        fetch(i + 1)

    slabs(i, lambda cp: cp.wait())

    srow = lax.broadcasted_iota(jnp.int32, (tm, ns), 1).astype(F32)
    weights = jnp.where(srow == info_ref[:, 0:1], info_ref[:, 2:3],
                        jnp.where(srow == info_ref[:, 1:2], info_ref[:, 3:4], 0.0))
    y = _dot(weights.astype(BF16), ys_scr[i % 2])
    x4 = x_ref[...] + mod_ref[0, 5:6, :] * y
    o_ref[...] = x4 * lax.rsqrt(jnp.mean(x4 * x4, axis=-1, keepdims=True) + RMS_EPS) * fg_ref[...]


def _combine(tbl, x3, mod, info, final_g, yr, seq, tm, ns, n_experts):
    t, d = x3.shape
    kern = functools.partial(_combine_kernel, n_experts=n_experts)
    return pl.pallas_call(
        kern,
        grid_spec=pltpu.PrefetchScalarGridSpec(
            num_scalar_prefetch=1,
            grid=(t // tm,),
            in_specs=[
                pl.BlockSpec((tm, d), lambda i, tb: (i, 0)),
                pl.BlockSpec((1, 6, d), _row_block(tm, seq)),
                pl.BlockSpec((tm, LANES), lambda i, tb: (i, 0)),
                pl.BlockSpec((1, d), lambda i, tb: (0, 0)),
                pl.BlockSpec(memory_space=pl.ANY),
            ],
            out_specs=pl.BlockSpec((tm, d), lambda i, tb: (i, 0)),
            scratch_shapes=[pltpu.VMEM((2, ns, d), BF16), pltpu.SemaphoreType.DMA((2,))],
        ),
        out_shape=jax.ShapeDtypeStruct((t, d), F32),
        compiler_params=_cparams(("arbitrary",), 40),
        name="moe_combine_final",
    )(tbl, x3, mod, info, final_g.reshape(1, d), yr)


def _moe(x3, mod, hb, info, tbl, w1, w3, w2, final_g, seq, tm):
    t, d = x3.shape
    n_experts = w1.shape[0]
    n_blk = t // tm
    ns = -(-(TOP_K * tm + n_experts * (ROW_ALIGN - 1)) // LANES) * LANES
    steps = (TOP_K * t + n_experts * n_blk * (ROW_ALIGN - 1)) // MOE_PAD + n_experts
    end = tbl[-1, 1, :n_experts] + tbl[-1, 0, :n_experts]
    nb = (end + MOE_PAD - 1) // MOE_PAD
    cum = jnp.cumsum(nb)
    n_active = cum[-1:].astype(jnp.int32)
    step = jnp.arange(steps, dtype=jnp.int32)
    owner = jnp.sum(jnp.minimum(step, n_active[0] - 1)[:, None] >= cum[None, :], axis=1)
    blk_expert = jnp.minimum(owner, n_experts - 1).astype(jnp.int32)
    flat = jnp.concatenate([tbl[:, :3, :n_experts].reshape(-1), (cum - nb) * MOE_PAD, n_active * MOE_PAD])
    flat = flat.astype(jnp.int32)
    xr = _dispatch(flat, hb, info, tm, ns, steps * MOE_PAD, n_experts)
    yr = _experts(blk_expert, n_active, xr, w1.astype(BF16), w3.astype(BF16), w2.astype(BF16),
                  w1.shape[2] // 2)
    return _combine(flat, x3, mod, info, final_g, yr, seq, tm, ns, n_experts)


def kernel(x, c, ada_w, ada_b, norm_mix_g, norm_ffn_g, final_g, ab_w_in, sgu_v_g, sgu_w_s, sgu_b_s, s5_lam_re, s5_lam_im, s5_log_dt, s5_b_re, s5_b_im, s5_c_re, s5_c_im, s5_d, s5_w_glu, ab_w_out, ffn_w1, ffn_w3, ffn_w2, mla_w_in, mla_q_norm_g, mla_kv_norm_g, mla_w_uq, mla_w_ukv, mla_w_o, moe_w_router, moe_w1, moe_w3, moe_w2):
    bn, seq, d = x.shape
    t = bn * seq
    tm = min(512, seq)
    x2 = x.reshape(t, d)
    mod = _adaln(c, ada_w, ada_b)

    a_out, u_tiles = _l0_in(x2, mod[0], norm_mix_g[0], ab_w_in[0], sgu_w_s[0], sgu_b_s[0], sgu_v_g[0], seq, tm)
    t_mat, ws, wy, aq = _s5_weights(s5_lam_re[0], s5_lam_im[0], s5_log_dt[0], s5_b_re[0], s5_b_im[0],
                                    s5_c_re[0], s5_c_im[0], S5_Q)
    y5 = _s5(u_tiles, t_mat, ws, wy, aq, bn, S5_Q)
    x2 = _l0_out(x2, mod[0], a_out, y5, u_tiles, s5_d[0], s5_w_glu[0], ab_w_out[0], seq, tm)
    x2 = _ffn(x2, mod[0], norm_ffn_g[0], ffn_w1[0], ffn_w3[0], ffn_w2[0], seq, tm, ffn_w1.shape[2] // 2)

    qt, k, vt = _mla_proj(x2, mod[1], norm_mix_g[1], mla_w_in[0], mla_q_norm_g[0], mla_kv_norm_g[0],
                          mla_w_uq[0], mla_w_ukv[0], seq, tm)
    o = _attention(qt, k, vt, bn, seq, min(1024, seq), min(1024, seq // 2))
    x3, hb, info, tbl = _attn_out(x2, mod[1], o, mla_w_o[0], norm_ffn_g[1], moe_w_router[0], seq, tm)
    out = _moe(x3, mod[1], hb, info, tbl, moe_w1[0], moe_w3[0], moe_w2[0], final_g, seq, tm)
    return out.reshape(bn, seq, d)
```

```python
import functools
import math

import jax
import jax.numpy as jnp
from jax import lax
from jax.experimental import pallas as pl
from jax.experimental.pallas import tpu as pltpu

F32 = jnp.float32
BF16 = jnp.bfloat16
HIGHEST = lax.Precision.HIGHEST

LANES = 128
SUBLANES = 8
MIB = 1024 * 1024

RMS_EPS = 1e-6
C_HEADS = 8
QK_NOPE = 128
QK_ROPE = 64
V_HEAD = 128
ROPE_THETA = 10000.0
TOP_K = 2

S5_Q = 8
MOE_PAD = 512


def _cparams(semantics, vmem_mib):
    return pltpu.CompilerParams(dimension_semantics=semantics, vmem_limit_bytes=vmem_mib * MIB)


def _dot(a, b):
    return jnp.dot(a, b, preferred_element_type=F32)


def _norm_mod(x, g, shift, scale):
    y = x * lax.rsqrt(jnp.mean(x * x, axis=-1, keepdims=True) + RMS_EPS)
    return (y * g) * (1.0 + scale) + shift


def _row_block(tm, seq):
    per_batch = seq // tm
    return lambda i, *_: (i // per_batch, 0, 0)


def _adaln_kernel(c_ref, w_ref, b_ref, o_ref):
    c = c_ref[...]
    ca = c * jax.nn.sigmoid(c)
    o_ref[0] = jnp.dot(ca, w_ref[0], preferred_element_type=F32, precision=HIGHEST) + b_ref[0]


def _adaln(c, ada_w, ada_b):
    depth, d, n = ada_w.shape
    bn = c.shape[0]
    tn = n // 4
    c8 = jnp.zeros((SUBLANES, d), F32).at[:bn].set(c)
    out = pl.pallas_call(
        _adaln_kernel,
        grid=(depth, n // tn),
        in_specs=[
            pl.BlockSpec((SUBLANES, d), lambda l, j: (0, 0)),
            pl.BlockSpec((1, d, tn), lambda l, j: (l, 0, j)),
            pl.BlockSpec((1, 1, tn), lambda l, j: (l, 0, j)),
        ],
        out_specs=pl.BlockSpec((1, SUBLANES, tn), lambda l, j: (l, 0, j)),
        out_shape=jax.ShapeDtypeStruct((depth, SUBLANES, n), F32),
        compiler_params=_cparams(("parallel", "parallel"), 32),
        name="adaln",
    )(c8, ada_w, ada_b.reshape(depth, 1, n))
    return out[:, :bn].reshape(depth, bn, 6, d)


def _l0_in_kernel(x_ref, mod_ref, g_ref, win_ref, ws_ref, bs_ref, vg_ref, a_ref, u_ref, *,
                  heads, chunk, n_tiles):
    x = x_ref[...]
    h = _norm_mod(x, g_ref[...], mod_ref[0, 0:1, :], mod_ref[0, 1:2, :]).astype(BF16)
    z = _dot(h, win_ref[...])
    tm = x.shape[0]
    a_width = heads * LANES
    for g in range(heads):
        u = jax.nn.gelu(z[:, g * LANES:(g + 1) * LANES])
        v = jax.nn.gelu(z[:, a_width + g * LANES:a_width + (g + 1) * LANES])
        vn = v * lax.rsqrt(jnp.mean(v * v, axis=-1, keepdims=True) + RMS_EPS) * vg_ref[g:g + 1, :]
        vnb = vn.astype(BF16)
        bias = bs_ref[:, g:g + 1]
        for ci in range(tm // chunk):
            rows = slice(ci * chunk, (ci + 1) * chunk)
            s = _dot(ws_ref[g], vnb[rows]) + bias
            a_ref[rows, g * LANES:(g + 1) * LANES] = (u[rows] * s).astype(BF16)
    for j in range(n_tiles):
        u_ref[j] = z[:, 2 * a_width + j * LANES:2 * a_width + (j + 1) * LANES].astype(BF16)


def _l0_in(x2, mod, norm_g, w_in, w_s, b_s, v_g, seq, tm):
    t, d = x2.shape
    heads, chunk, _ = w_s.shape
    a_width = heads * LANES
    b_width = w_in.shape[1] - 2 * a_width
    n_tiles = b_width // LANES
    kern = functools.partial(_l0_in_kernel, heads=heads, chunk=chunk, n_tiles=n_tiles)
    return pl.pallas_call(
        kern,
        grid=(t // tm,),
        in_specs=[
            pl.BlockSpec((tm, d), lambda i: (i, 0)),
            pl.BlockSpec((1, 6, d), _row_block(tm, seq)),
            pl.BlockSpec((1, d), lambda i: (0, 0)),
            pl.BlockSpec(w_in.shape, lambda i: (0, 0)),
            pl.BlockSpec(w_s.shape, lambda i: (0, 0, 0)),
            pl.BlockSpec((chunk, heads), lambda i: (0, 0)),
            pl.BlockSpec((heads, LANES), lambda i: (0, 0)),
        ],
        out_specs=[
            pl.BlockSpec((tm, a_width), lambda i: (i, 0)),
            pl.BlockSpec((n_tiles, tm, LANES), lambda i: (0, i, 0)),
        ],
        out_shape=[
            jax.ShapeDtypeStruct((t, a_width), BF16),
            jax.ShapeDtypeStruct((n_tiles, t, LANES), BF16),
        ],
        compiler_params=_cparams(("parallel",), 40),
        name="l0_in_sgu",
    )(x2, mod, norm_g.reshape(1, d), w_in.astype(BF16), w_s.astype(BF16), b_s.T, v_g)


def _s5_weights(lam_re, lam_im, log_dt, b_re, b_im, c_re, c_im, q):
    _, groups, p = lam_re.shape
    h = b_re.shape[-1]
    gpt = LANES // h
    nt = groups // gpt
    dt = jnp.exp(log_dt.astype(F32))[..., None]
    mag = jnp.exp(lam_re * dt)
    ang = lam_im * dt
    ar, ai = mag * jnp.cos(ang), mag * jnp.sin(ang)
    den = lam_re * lam_re + lam_im * lam_im
    fr = ((ar - 1.0) * lam_re + ai * lam_im) / den
    fi = (ai * lam_re - (ar - 1.0) * lam_im) / den
    bbr = fr[..., None] * b_re - fi[..., None] * b_im
    bbi = fr[..., None] * b_im + fi[..., None] * b_re
    prs, pis = [jnp.ones_like(ar)], [jnp.zeros_like(ar)]
    for _ in range(q):
        prs.append(prs[-1] * ar - pis[-1] * ai)
        pis.append(prs[-2] * ai + pis[-1] * ar)
    pr, pi = jnp.stack(prs), jnp.stack(pis)
    abr = pr[..., None] * bbr - pi[..., None] * bbi
    abi = pr[..., None] * bbi + pi[..., None] * bbr
    car = c_re * pr[:, :, :, None, :] - c_im * pi[:, :, :, None, :]
    cai = c_re * pi[:, :, :, None, :] + c_im * pr[:, :, :, None, :]
    kk = (jnp.einsum('dgop,kdgpi->kdgoi', c_re, abr[:q], precision=HIGHEST)
          - jnp.einsum('dgop,kdgpi->kdgoi', c_im, abi[:q], precision=HIGHEST))
    idx = jnp.arange(q)
    dfi = idx[None, :] - idx[:, None]
    kf = jnp.where((dfi >= 0)[:, :, None, None, None], kk[:, 0][jnp.clip(dfi, 0, q - 1)], 0.0)
    kb = jnp.where((dfi <= 0)[:, :, None, None, None], kk[:, 1][jnp.clip(-dfi, 0, q - 1)], 0.0)
    m = (kf + kb).reshape(q, q, nt, gpt, h, h)
    w4 = jnp.stack([jnp.stack([abr[q - 1 - idx, 0], abi[q - 1 - idx, 0]]),
                    jnp.stack([abr[idx, 1], abi[idx, 1]])])
    w4 = w4.reshape(2, 2, q, nt, gpt, p, h)
    y4 = jnp.stack([jnp.stack([car[idx + 1, 0], -cai[idx + 1, 0]]),
                    jnp.stack([car[q - idx, 1], -cai[q - idx, 1]])])
    y4 = y4.reshape(2, 2, q, nt, gpt, h, p)

    a_in = m.transpose(2, 0, 3, 5, 1, 4).reshape(nt, q * LANES, q * h)
    a_ws = w4.transpose(3, 2, 4, 6, 0, 1, 5).reshape(nt, q * LANES, 4 * p)
    a_wy = y4.transpose(3, 0, 1, 4, 6, 2, 5).reshape(nt, 4 * gpt * p, q * h)

    def expand(a, col_src, row_group, col_group):
        pick = (jnp.arange(a.shape[2])[:, None] == col_src[None, :]).astype(BF16)
        wide = jnp.einsum('jrk,kc->jrc', a.astype(BF16), pick, preferred_element_type=F32)
        return jnp.where(row_group[:, None] == col_group[None, :], wide, 0.0).astype(BF16)

    r_in = jnp.arange(q * LANES)
    r_st = jnp.arange(4 * gpt * p)
    g_in = (r_in // h) % gpt
    g_st = (r_st // p) % gpt
    src_in = (r_in // LANES) * h + r_in % h
    src_st = (r_st // (gpt * p)) * p + r_st % p
    t_mat = expand(a_in, src_in, g_in, g_in)
    ws = expand(a_ws, src_st, g_in, g_st)
    wy = expand(a_wy, src_in, g_st, g_in)

    aq = jnp.stack([jnp.stack([pr[q, 0], pi[q, 0]]), jnp.stack([pr[q, 1], pi[q, 1]])])
    aq = aq.reshape(2, 2, nt, gpt, p).transpose(2, 0, 1, 3, 4).reshape(nt, 1, 4 * gpt * p)
    return t_mat, ws, wy, aq


def _s5_kernel(u_ref, t_ref, ws_ref, wy_ref, aq_ref, y_ref, s_ref):
    u = u_ref[0]
    s_ref[...] = _dot(u, ws_ref[0])
    rows, width = s_ref.shape
    hw = width // 4
    aq = aq_ref[0]
    afr, afi = aq[:, 0:hw], aq[:, hw:2 * hw]
    abr, abi = aq[:, 2 * hw:3 * hw], aq[:, 3 * hw:4 * hw]

    def body(c, carry):
        xfr, xfi, xbr, xbi = carry
        rf = pl.ds(c, 1)
        rb = pl.ds(rows - 1 - c, 1)
        sfr = s_ref[rf, 0:hw]
        sfi = s_ref[rf, hw:2 * hw]
        sbr = s_ref[rb, 2 * hw:3 * hw]
        sbi = s_ref[rb, 3 * hw:4 * hw]
        s_ref[rf, 0:hw] = xfr
        s_ref[rf, hw:2 * hw] = xfi
        s_ref[rb, 2 * hw:3 * hw] = xbr
        s_ref[rb, 3 * hw:4 * hw] = xbi
        return (afr * xfr - afi * xfi + sfr, afr * xfi + afi * xfr + sfi,
                abr * xbr - abi * xbi + sbr, abr * xbi + abi * xbr + sbi)

    zero = jnp.zeros((1, hw), F32)
    lax.fori_loop(0, rows, body, (zero, zero, zero, zero), unroll=8)
    y_ref[0] = _dot(u, t_ref[0]) + _dot(s_ref[...].astype(BF16), wy_ref[0])


def _s5(u_tiles, t_mat, ws, wy, aq, bn, q):
    nt, t, _ = u_tiles.shape
    rows = t // q // bn
    u2 = u_tiles.reshape(nt, t // q, q * LANES)
    sw = ws.shape[2]
    y2 = pl.pallas_call(
        _s5_kernel,
        grid=(nt, bn),
        in_specs=[
            pl.BlockSpec((1, rows, q * LANES), lambda j, b: (j, b, 0)),
            pl.BlockSpec((1,) + t_mat.shape[1:], lambda j, b: (j, 0, 0)),
            pl.BlockSpec((1,) + ws.shape[1:], lambda j, b: (j, 0, 0)),
            pl.BlockSpec((1,) + wy.shape[1:], lambda j, b: (j, 0, 0)),
            pl.BlockSpec((1, 1, sw), lambda j, b: (j, 0, 0)),
        ],
        out_specs=pl.BlockSpec((1, rows, q * LANES), lambda j, b: (j, b, 0)),
        out_shape=jax.ShapeDtypeStruct((nt, t // q, q * LANES), F32),
        scratch_shapes=[pltpu.VMEM((rows, sw), F32)],
        compiler_params=_cparams(("parallel", "parallel"), 48),
        name="s5_chunked",
    )(u2, t_mat, ws, wy, aq)
    return y2.reshape(nt, t, LANES)


def _l0_out_kernel(x_ref, mod_ref, a_ref, y5_ref, u_ref, d_ref, wglu_ref, wout_ref, o_ref, *, n_tiles):
    ys = []
    for j in range(n_tiles):
        ys.append(jax.nn.gelu(y5_ref[j] + d_ref[j] * u_ref[j].astype(F32)))
    y = jnp.concatenate(ys, axis=-1)
    glu = jax.nn.sigmoid(_dot(y.astype(BF16), wglu_ref[...]))
    b_out = (y * glu).astype(BF16)
    a_width = a_ref.shape[1]
    out = _dot(a_ref[...], wout_ref[0:a_width, :]) + _dot(b_out, wout_ref[a_width:, :])
    o_ref[...] = x_ref[...] + mod_ref[0, 2:3, :] * out


def _l0_out(x2, mod, a_out, y5, u_tiles, d_skip, w_glu, w_out, seq, tm):
    t, d = x2.shape
    n_tiles = u_tiles.shape[0]
    a_width = a_out.shape[1]
    kern = functools.partial(_l0_out_kernel, n_tiles=n_tiles)
    return pl.pallas_call(
        kern,
        grid=(t // tm,),
        in_specs=[
            pl.BlockSpec((tm, d), lambda i: (i, 0)),
            pl.BlockSpec((1, 6, d), _row_block(tm, seq)),
            pl.BlockSpec((tm, a_width), lambda i: (i, 0)),
            pl.BlockSpec((n_tiles, tm, LANES), lambda i: (0, i, 0)),
            pl.BlockSpec((n_tiles, tm, LANES), lambda i: (0, i, 0)),
            pl.BlockSpec((n_tiles, 1, LANES), lambda i: (0, 0, 0)),
            pl.BlockSpec(w_glu.shape, lambda i: (0, 0)),
            pl.BlockSpec(w_out.shape, lambda i: (0, 0)),
        ],
        out_specs=pl.BlockSpec((tm, d), lambda i: (i, 0)),
        out_shape=jax.ShapeDtypeStruct((t, d), F32),
        compiler_params=_cparams(("parallel",), 40),
        name="l0_out",
    )(x2, mod, a_out, y5, u_tiles, d_skip.reshape(n_tiles, 1, LANES),
      w_glu.astype(BF16), w_out.astype(BF16))


def _ffn_kernel(x_ref, mod_ref, g_ref, w1_ref, w3_ref, w2_ref, o_ref, h_scr, acc_scr):
    f = pl.program_id(1)

    @pl.when(f == 0)
    def _():
        h_scr[...] = _norm_mod(x_ref[...], g_ref[...], mod_ref[0, 3:4, :], mod_ref[0, 4:5, :]).astype(BF16)
        acc_scr[...] = jnp.zeros_like(acc_scr)

    h = h_scr[...]
    a = _dot(h, w1_ref[...])
    b = _dot(h, w3_ref[...])
    act = (a * jax.nn.sigmoid(a) * b).astype(BF16)
    acc_scr[...] += _dot(act, w2_ref[...])

    @pl.when(f == pl.num_programs(1) - 1)
    def _():
        o_ref[...] = x_ref[...] + mod_ref[0, 5:6, :] * acc_scr[...]


def _ffn(x2, mod, norm_g, w1, w3, w2, seq, tm, tf):
    t, d = x2.shape
    ff = w1.shape[1]
    return pl.pallas_call(
        _ffn_kernel,
        grid=(t // tm, ff // tf),
        in_specs=[
            pl.BlockSpec((tm, d), lambda i, f: (i, 0)),
            pl.BlockSpec((1, 6, d), _row_block(tm, seq)),
            pl.BlockSpec((1, d), lambda i, f: (0, 0)),
            pl.BlockSpec((d, tf), lambda i, f: (0, f)),
            pl.BlockSpec((d, tf), lambda i, f: (0, f)),
            pl.BlockSpec((tf, d), lambda i, f: (f, 0)),
        ],
        out_specs=pl.BlockSpec((tm, d), lambda i, f: (i, 0)),
        out_shape=jax.ShapeDtypeStruct((t, d), F32),
        scratch_shapes=[pltpu.VMEM((tm, d), BF16), pltpu.VMEM((tm, d), F32)],
        compiler_params=_cparams(("parallel", "arbitrary"), 48),
        name="ffn_swiglu",
    )(x2, mod, norm_g.reshape(1, d), w1.astype(BF16), w3.astype(BF16), w2.astype(BF16))


_NT_DIMS = (((1,), (1,)), ((), ()))
_TN_DIMS = (((0,), (0,)), ((), ()))


def _mla_proj_kernel(x_ref, mod_ref, g_ref, win_ref, gq_ref, gkv_ref, wqt_ref, wkn_ref, wvt_ref,
                     cos_ref, sin_ref, cost_ref, sint_ref, qt_ref, k_ref, vt_ref, *, q_lora, kv_lora, heads):
    h = _norm_mod(x_ref[...], g_ref[...], mod_ref[0, 0:1, :], mod_ref[0, 1:2, :]).astype(BF16)
    z = _dot(h, win_ref[...])
    cq = z[:, :q_lora]
    cq = (cq * lax.rsqrt(jnp.mean(cq * cq, axis=-1, keepdims=True) + RMS_EPS) * gq_ref[...]).astype(BF16)
    ckv = z[:, q_lora:q_lora + kv_lora]
    ckv = (ckv * lax.rsqrt(jnp.mean(ckv * ckv, axis=-1, keepdims=True) + RMS_EPS) * gkv_ref[...]).astype(BF16)
    r0 = q_lora + kv_lora
    k_rope = (z[:, r0:r0 + LANES] * cos_ref[...] + z[:, r0 + LANES:r0 + 2 * LANES] * sin_ref[...]).astype(BF16)
    kn = _dot(ckv, wkn_ref[...])
    qat = lax.dot_general(wqt_ref[...], cq, _NT_DIMS, preferred_element_type=F32)
    vt_ref[0] = lax.dot_general(wvt_ref[...], ckv, _NT_DIMS, preferred_element_type=F32).astype(BF16)
    cost = cost_ref[...]
    sint = sint_ref[...]
    hw = heads * LANES
    for hd in range(heads):
        c = slice(hd * LANES, (hd + 1) * LANES)
        qt_ref[2 * hd * LANES:(2 * hd + 1) * LANES, :] = qat[c, :].astype(BF16)
        q_rope = qat[hw + hd * LANES:hw + (hd + 1) * LANES, :] * cost \
            + qat[2 * hw + hd * LANES:2 * hw + (hd + 1) * LANES, :] * sint
        qt_ref[(2 * hd + 1) * LANES:(2 * hd + 2) * LANES, :] = q_rope.astype(BF16)
        k_ref[:, 2 * hd * LANES:(2 * hd + 1) * LANES] = kn[:, c].astype(BF16)
        k_ref[:, (2 * hd + 1) * LANES:(2 * hd + 2) * LANES] = k_rope


def _rope_pad(w_rope):
    half = QK_ROPE // 2
    x1, x2 = w_rope[..., :half], w_rope[..., half:]
    zeros = jnp.zeros(w_rope.shape[:-1] + (LANES - QK_ROPE,), w_rope.dtype)
    return jnp.concatenate([x1, x2, zeros], -1), jnp.concatenate([-x2, x1, zeros], -1)


def _mla_proj(x2, mod, norm_g, w_in, gq, gkv, w_uq, w_ukv, seq, tm):
    t, d = x2.shape
    q_lora, kv_lora = gq.shape[0], gkv.shape[0]
    heads = C_HEADS
    scale = math.log2(math.e) / math.sqrt(QK_NOPE + QK_ROPE)
    kr, krs = _rope_pad(w_in[:, q_lora + kv_lora:])
    w_in_ext = jnp.concatenate([w_in[:, :q_lora + kv_lora], kr, krs], -1).astype(BF16)
    wq3 = (w_uq * scale).reshape(q_lora, heads, QK_NOPE + QK_ROPE)
    qr, qrs = _rope_pad(wq3[..., QK_NOPE:])
    wqt = jnp.concatenate([wq3[..., :QK_NOPE].reshape(q_lora, -1), qr.reshape(q_lora, -1),
                           qrs.reshape(q_lora, -1)], -1).T.astype(BF16)
    wkv3 = w_ukv.reshape(kv_lora, heads, QK_NOPE + V_HEAD)
    wkn = wkv3[..., :QK_NOPE].reshape(kv_lora, -1).astype(BF16)
    wvt = wkv3[..., QK_NOPE:].reshape(kv_lora, -1).T.astype(BF16)
    inv_freq = ROPE_THETA ** (-jnp.arange(0, QK_ROPE, 2, dtype=F32) / QK_ROPE)
    ang = jnp.arange(seq, dtype=F32)[:, None] * inv_freq[None, :]
    pad = jnp.zeros((seq, LANES - QK_ROPE), F32)
    cos = jnp.concatenate([jnp.cos(ang), jnp.cos(ang), pad], -1)
    sin = jnp.concatenate([jnp.sin(ang), jnp.sin(ang), pad], -1)
    per_batch = seq // tm
    kern = functools.partial(_mla_proj_kernel, q_lora=q_lora, kv_lora=kv_lora, heads=heads)
    const = lambda i: (0, 0)
    return pl.pallas_call(
        kern,
        grid=(t // tm,),
        in_specs=[
            pl.BlockSpec((tm, d), lambda i: (i, 0)),
            pl.BlockSpec((1, 6, d), _row_block(tm, seq)),
            pl.BlockSpec((1, d), const),
            pl.BlockSpec(w_in_ext.shape, const),
            pl.BlockSpec((1, q_lora), const),
            pl.BlockSpec((1, kv_lora), const),
            pl.BlockSpec(wqt.shape, const),
            pl.BlockSpec(wkn.shape, const),
            pl.BlockSpec(wvt.shape, const),
            pl.BlockSpec((tm, LANES), lambda i: (i % per_batch, 0)),
            pl.BlockSpec((tm, LANES), lambda i: (i % per_batch, 0)),
            pl.BlockSpec((LANES, tm), lambda i: (0, i % per_batch)),
            pl.BlockSpec((LANES, tm), lambda i: (0, i % per_batch)),
        ],
        out_specs=[
            pl.BlockSpec((2 * heads * LANES, tm), lambda i: (0, i)),
            pl.BlockSpec((tm, 2 * heads * LANES), lambda i: (i, 0)),
            pl.BlockSpec((1, heads * V_HEAD, tm), lambda i: (i, 0, 0)),
        ],
        out_shape=[
            jax.ShapeDtypeStruct((2 * heads * LANES, t), BF16),
            jax.ShapeDtypeStruct((t, 2 * heads * LANES), BF16),
            jax.ShapeDtypeStruct((t // tm, heads * V_HEAD, tm), BF16),
        ],
        compiler_params=_cparams(("parallel",), 48),
        name="mla_proj",
    )(x2, mod, norm_g.reshape(1, d), w_in_ext, gq.reshape(1, -1), gkv.reshape(1, -1), wqt, wkn, wvt,
      cos, sin, cos.T, sin.T)


def _attn_kernel(qt_ref, k_ref, vt_ref, w1_ref, w3_ref, w2_ref, o_ref, w1b_ref, w3b_ref, w2b_ref,
                 s_scr, acc_scr, *, tk, per_trip):
    w1b_ref[...] = w1_ref[...].astype(BF16)
    w3b_ref[...] = w3_ref[...].astype(BF16)
    w2b_ref[...] = w2_ref[...].astype(BF16)

    tq = qt_ref.shape[1]
    n_kv = k_ref.shape[0] // tk
    tv = vt_ref.shape[2]
    sub = tk // tv

    def scores(c, slot):
        k = k_ref[pl.ds(pl.multiple_of(c * tk, tk), tk), :]
        s_scr[slot] = _dot(k, qt_ref[...])

    def update(c, slot, m, l):
        s = s_scr[slot]
        m_new = jnp.maximum(m, jnp.max(s, axis=0, keepdims=True))
        alpha = jnp.exp2(m - m_new)
        p = jnp.exp2(s - m_new)
        l = alpha * l + jnp.sum(p, axis=0, keepdims=True)
        pb = p.astype(BF16)
        acc = alpha * acc_scr[...]
        for j in range(sub):
            acc = acc + _dot(vt_ref[c * sub + j], pb[j * tv:(j + 1) * tv])
        acc_scr[...] = acc
        return m_new, l

    def steps(c0, count, m, l, final):
        for u in range(count):
            if not (final and u == count - 1):
                scores(c0 + u + 1, (u + 1) % 2)
            m, l = update(c0 + u, u % 2, m, l)
        return m, l

    def body(trip, carry):
        return steps(trip * per_trip, per_trip, *carry, final=False)

    acc_scr[...] = jnp.zeros_like(acc_scr)
    scores(0, 0)
    trips = n_kv // per_trip - 1
    m, l = lax.fori_loop(0, trips, body, (jnp.full((1, tq), -jnp.inf, F32), jnp.zeros((1, tq), F32)))
    m, l = steps(trips * per_trip, per_trip, m, l, final=True)
    o_ref[0] = (acc_scr[...] / l).astype(BF16)


def _attention(qt, k, vt, bn, seq, tq, tk, w1, w3, w2):
    tv = vt.shape[2]
    t = k.shape[0]
    heads = C_HEADS
    nq = seq // tq
    n_steps = bn * heads * nq
    kern = functools.partial(_attn_kernel, tk=tk, per_trip=2)

    def step(b, h, i):
        return ((b * heads + h) * nq + i, 0)

    flat = [w.reshape(-1, w.shape[-1]) for w in (w1, w3, w2)]
    w_specs = [pl.BlockSpec((w.shape[0] // n_steps, w.shape[1]), step) for w in flat]
    for w in flat:
        assert w.shape[0] % (n_steps * ROW_ALIGN) == 0, "weight rows must split evenly over the grid steps"
    o, w1b, w3b, w2b = pl.pallas_call(
        kern,
        grid=(bn, heads, nq),
        in_specs=[
            pl.BlockSpec((2 * LANES, tq), lambda b, h, i: (h, b * nq + i)),
            pl.BlockSpec((seq, 2 * LANES), lambda b, h, i: (b, h)),
            pl.BlockSpec((seq // tv, V_HEAD, tv), lambda b, h, i: (b, h, 0)),
        ] + w_specs,
        out_specs=[pl.BlockSpec((1, V_HEAD, tq), lambda b, h, i: (b * nq + i, h, 0))] + w_specs,
        out_shape=[jax.ShapeDtypeStruct((t // tq, heads * V_HEAD, tq), BF16)]
        + [jax.ShapeDtypeStruct(w.shape, BF16) for w in flat],
        scratch_shapes=[pltpu.VMEM((2, tk, tq), F32), pltpu.VMEM((V_HEAD, tq), F32)],
        compiler_params=_cparams(("parallel", "parallel", "parallel"), 48),
        name="mla_attention",
    )(qt, k, vt, *flat)
    return o, w1b.reshape(w1.shape), w3b.reshape(w3.shape), w2b.reshape(w2.shape)


ROW_ALIGN = 16
SLAB_SIZES = (512, 256, 128, 64, 32, 16)


def _slab_copies(src_ref, src_row, dst_ref, dst_row, n_rows, sem, fn):
    done = 0
    for size in SLAB_SIZES:
        take = (n_rows & size) != 0
        s0 = pl.multiple_of(src_row + done, ROW_ALIGN)
        d0 = pl.multiple_of(dst_row + done, ROW_ALIGN)

        @pl.when(take)
        def _(s0=s0, d0=d0, size=size):
            fn(pltpu.make_async_copy(src_ref.at[pl.ds(s0, size), :], dst_ref.at[pl.ds(d0, size), :], sem))

        done = done + jnp.where(take, size, 0)


def _attn_out_kernel(x_ref, mod_ref, o_ref, wo_ref, g_ref, wr_ref, x3_ref, h_ref, info_ref, tbl_ref,
                     run_scr, *, n_experts):
    @pl.when(pl.program_id(0) == 0)
    def _():
        run_scr[...] = jnp.zeros_like(run_scr)

    y = lax.dot_general(o_ref[0], wo_ref[...], _TN_DIMS, preferred_element_type=F32)
    x3 = x_ref[...] + mod_ref[0, 2:3, :] * y
    x3_ref[...] = x3
    h = _norm_mod(x3, g_ref[...], mod_ref[0, 3:4, :], mod_ref[0, 4:5, :])
    h_ref[...] = h.astype(BF16)
    tm, d = h.shape

    h_hi = h.astype(BF16)
    h_lo = (h - h_hi.astype(F32)).astype(BF16)
    wr = wr_ref[...]
    w_hi = wr.astype(BF16)
    w_lo = (wr - w_hi.astype(F32)).astype(BF16)
    logits = _dot(h_hi, w_hi) + (_dot(h_hi, w_lo) + _dot(h_lo, w_hi))
    lane = lax.broadcasted_iota(jnp.int32, (tm, LANES), 1).astype(F32)
    neg = jnp.float32(-jnp.inf)
    lg = jnp.where(lane < n_experts, logits, neg)
    m1 = jnp.max(lg, axis=-1, keepdims=True)
    i1 = jnp.min(jnp.where(lg == m1, lane, float(LANES)), axis=-1, keepdims=True)
    lg2 = jnp.where(lane == i1, neg, lg)
    m2 = jnp.max(lg2, axis=-1, keepdims=True)
    i2 = jnp.min(jnp.where(lg2 == m2, lane, float(LANES)), axis=-1, keepdims=True)
    e = jnp.exp(m2 - m1)
    g0 = 1.0 / (1.0 + e)
    g1 = e / (1.0 + e)
    sel1 = lane == i1
    sel2 = lane == i2
    onehot = jnp.where(sel1, 1.0, 0.0) + jnp.where(sel2, 1.0, 0.0)
    row = lax.broadcasted_iota(jnp.int32, (tm, tm), 0)
    col = lax.broadcasted_iota(jnp.int32, (tm, tm), 1)
    tri = jnp.where(col < row, 1.0, 0.0).astype(BF16)
    before = _dot(tri, onehot.astype(BF16))
    n_pick = jnp.sum(onehot, axis=0, keepdims=True)
    n_slab = jnp.floor((n_pick + (ROW_ALIGN - 1)) * (1.0 / ROW_ALIGN)) * ROW_ALIGN
    lrow = lax.broadcasted_iota(jnp.int32, (LANES, LANES), 0)
    lcol = lax.broadcasted_iota(jnp.int32, (LANES, LANES), 1)
    upper = jnp.where(lrow < lcol, 1.0, 0.0).astype(BF16)
    n8 = jnp.broadcast_to(n_slab, (SUBLANES, LANES))
    loc = _dot(n8.astype(BF16), upper)[0:1]
    where = before + loc
    pos0 = jnp.sum(jnp.where(sel1, where, 0.0), axis=-1, keepdims=True)
    pos1 = jnp.sum(jnp.where(sel2, where, 0.0), axis=-1, keepdims=True)
    info_ref[...] = jnp.where(lane == 0, pos0, jnp.where(lane == 1, pos1, jnp.where(
        lane == 2, g0, jnp.where(lane == 3, g1, 0.0))))

    off = run_scr[...]
    run_scr[...] = off + n_slab
    trow = lax.broadcasted_iota(jnp.int32, (SUBLANES, LANES), 0)
    tbl = jnp.where(trow == 0, n_slab, jnp.where(trow == 1, off, jnp.where(trow == 2, loc, 0.0)))
    tbl_ref[0] = tbl.astype(jnp.int32)


def _attn_out(x2, mod, o, w_o, norm_g, w_router, seq, tm):
    t, d = x2.shape
    n_experts = w_router.shape[1]
    wr = jnp.zeros((d, LANES), F32).at[:, :n_experts].set(w_router)
    kern = functools.partial(_attn_out_kernel, n_experts=n_experts)
    o_split = o.shape[2] // tm
    const = lambda i: (0, 0)
    return pl.pallas_call(
        kern,
        grid=(t // tm,),
        in_specs=[
            pl.BlockSpec((tm, d), lambda i: (i, 0)),
            pl.BlockSpec((1, 6, d), _row_block(tm, seq)),
            pl.BlockSpec((1, o.shape[1], tm), lambda i: (i // o_split, 0, i % o_split)),
            pl.BlockSpec(w_o.shape, const),
            pl.BlockSpec((1, d), const),
            pl.BlockSpec((d, LANES), const),
        ],
        out_specs=[
            pl.BlockSpec((tm, d), lambda i: (i, 0)),
            pl.BlockSpec((tm, d), lambda i: (i, 0)),
            pl.BlockSpec((tm, LANES), lambda i: (i, 0)),
            pl.BlockSpec((1, SUBLANES, LANES), lambda i: (i, 0, 0)),
        ],
        out_shape=[
            jax.ShapeDtypeStruct((t, d), F32),
            jax.ShapeDtypeStruct((t, d), BF16),
            jax.ShapeDtypeStruct((t, LANES), F32),
            jax.ShapeDtypeStruct((t // tm, SUBLANES, LANES), jnp.int32),
        ],
        scratch_shapes=[pltpu.VMEM((1, LANES), F32)],
        compiler_params=_cparams(("arbitrary",), 40),
        name="attn_out_router",
    )(x2, mod, o, w_o.astype(BF16), norm_g.reshape(1, d), wr)


def _dispatch_kernel(tbl_ref, h_ref, info_ref, xr_hbm, xs_scr, zero_scr, sem, *, n_experts, n_fill):
    i = pl.program_id(0)
    n_blk = pl.num_programs(0)
    tm = h_ref.shape[0]
    ns = xs_scr.shape[0]
    base = i * (3 * n_experts)
    starts = n_blk * (3 * n_experts)

    srow = lax.broadcasted_iota(jnp.int32, (tm, ns), 1).astype(F32)
    pick = jnp.logical_or(srow == info_ref[:, 0:1], srow == info_ref[:, 1:2])
    pick = jnp.where(pick, 1.0, 0.0).astype(BF16)
    xs_scr[...] = lax.dot_general(pick, h_ref[...], _TN_DIMS, preferred_element_type=F32).astype(BF16)

    def slabs(fn):
        for e in range(n_experts):
            dst = tbl_ref[starts + e] + tbl_ref[base + n_experts + e]
            _slab_copies(xs_scr, tbl_ref[base + 2 * n_experts + e], xr_hbm, dst, tbl_ref[base + e], sem, fn)

    slabs(lambda cp: cp.start())
    slabs(lambda cp: cp.wait())

    @pl.when(i == n_blk - 1)
    def _():
        zero_scr[...] = jnp.zeros_like(zero_scr)
        used = tbl_ref[starts + n_experts]

        def fills(fn):
            for e in range(n_experts):
                end = tbl_ref[starts + e] + tbl_ref[base + n_experts + e] + tbl_ref[base + e]
                _slab_copies(zero_scr, 0, xr_hbm, end, (-end) & (MOE_PAD - 1), sem, fn)
            for j in range(n_fill):
                row = pl.multiple_of(used + j * MOE_PAD, MOE_PAD)

                @pl.when(row < xr_hbm.shape[0])
                def _(row=row):
                    fn(pltpu.make_async_copy(zero_scr, xr_hbm.at[pl.ds(row, MOE_PAD), :], sem))

        fills(lambda cp: cp.start())
        fills(lambda cp: cp.wait())


def _dispatch(tbl, hb, info, tm, ns, n_rows, n_experts):
    t, d = hb.shape
    n_fill = n_rows // MOE_PAD - (TOP_K * t) // MOE_PAD
    kern = functools.partial(_dispatch_kernel, n_experts=n_experts, n_fill=n_fill)
    return pl.pallas_call(
        kern,
        grid_spec=pltpu.PrefetchScalarGridSpec(
            num_scalar_prefetch=1,
            grid=(t // tm,),
            in_specs=[
                pl.BlockSpec((tm, d), lambda i, tb: (i, 0)),
                pl.BlockSpec((tm, LANES), lambda i, tb: (i, 0)),
            ],
            out_specs=pl.BlockSpec(memory_space=pl.ANY),
            scratch_shapes=[pltpu.VMEM((ns, d), BF16), pltpu.VMEM((MOE_PAD, d), BF16),
                            pltpu.SemaphoreType.DMA(())],
        ),
        out_shape=jax.ShapeDtypeStruct((n_rows, d), BF16),
        compiler_params=_cparams(("arbitrary",), 40),
        name="moe_dispatch",
    )(tbl, hb, info)


def _experts_kernel(be_ref, na_ref, x_ref, w1_ref, w3_ref, w2_ref, o_ref, acc_scr):
    b = pl.program_id(0)
    f = pl.program_id(1)
    active = b < na_ref[0]
    last = f == pl.num_programs(1) - 1

    @pl.when(jnp.logical_and(active, f == 0))
    def _():
        acc_scr[...] = jnp.zeros_like(acc_scr)

    @pl.when(active)
    def _():
        x = x_ref[...]
        a = _dot(x, w1_ref[0])
        g = _dot(x, w3_ref[0])
        act = (a * jax.nn.sigmoid(a) * g).astype(BF16)
        acc_scr[...] += _dot(act, w2_ref[0])

    @pl.when(jnp.logical_and(active, last))
    def _():
        o_ref[...] = acc_scr[...].astype(BF16)

    @pl.when(jnp.logical_and(jnp.logical_not(active), last))
    def _():
        o_ref[...] = jnp.zeros_like(o_ref)


def _experts(blk_expert, n_active, xr, w1, w3, w2, tf):
    rows = MOE_PAD
    _, d, ff = w1.shape
    last_f = ff // tf - 1

    def ftile(b, f, na):
        return jnp.where(b < na[0], f, last_f)

    return pl.pallas_call(
        _experts_kernel,
        grid_spec=pltpu.PrefetchScalarGridSpec(
            num_scalar_prefetch=2,
            grid=(xr.shape[0] // rows, ff // tf),
            in_specs=[
                pl.BlockSpec((rows, d), lambda b, f, be, na: (b, 0)),
                pl.BlockSpec((1, d, tf), lambda b, f, be, na: (be[b], 0, ftile(b, f, na))),
                pl.BlockSpec((1, d, tf), lambda b, f, be, na: (be[b], 0, ftile(b, f, na))),
                pl.BlockSpec((1, tf, d), lambda b, f, be, na: (be[b], ftile(b, f, na), 0)),
            ],
            out_specs=pl.BlockSpec((rows, d), lambda b, f, be, na: (b, 0)),
            scratch_shapes=[pltpu.VMEM((rows, d), F32)],
        ),
        out_shape=jax.ShapeDtypeStruct(xr.shape, BF16),
        compiler_params=_cparams(("parallel", "arbitrary"), 48),
        name="moe_experts",
    )(blk_expert, n_active, xr, w1, w3, w2)


def _combine_kernel(tbl_ref, x_ref, mod_ref, info_ref, fg_ref, y_hbm, o_ref, ys_scr, sem, *, n_experts):
    tm, d = x_ref.shape
    ns = ys_scr.shape[1]
    i = pl.program_id(0)
    n_blk = pl.num_programs(0)
    starts = n_blk * (3 * n_experts)

    def slabs(blk, fn):
        slot = blk % 2
        base = blk * (3 * n_experts)
        for e in range(n_experts):
            src = tbl_ref[starts + e] + tbl_ref[base + n_experts + e]
            _slab_copies(y_hbm, src, ys_scr.at[slot], tbl_ref[base + 2 * n_experts + e], tbl_ref[base + e],
                         sem.at[slot], fn)

    def fetch(blk):
        ys_scr[blk % 2] = jnp.zeros((ns, d), BF16)
        slabs(blk, lambda cp: cp.start())

    @pl.when(i == 0)
    def _():
        fetch(i)

    @pl.when(i + 1 < n_blk)
    def _():
        fetch(i + 1)

    slabs(i, lambda cp: cp.wait())

    srow = lax.broadcasted_iota(jnp.int32, (tm, ns), 1).astype(F32)
    weights = jnp.where(srow == info_ref[:, 0:1], info_ref[:, 2:3],
                        jnp.where(srow == info_ref[:, 1:2], info_ref[:, 3:4], 0.0))
    y = _dot(weights.astype(BF16), ys_scr[i % 2])
    x4 = x_ref[...] + mod_ref[0, 5:6, :] * y
    o_ref[...] = x4 * lax.rsqrt(jnp.mean(x4 * x4, axis=-1, keepdims=True) + RMS_EPS) * fg_ref[...]


def _combine(tbl, x3, mod, info, final_g, yr, seq, tm, ns, n_experts):
    t, d = x3.shape
    kern = functools.partial(_combine_kernel, n_experts=n_experts)
    return pl.pallas_call(
        kern,
        grid_spec=pltpu.PrefetchScalarGridSpec(
            num_scalar_prefetch=1,
            grid=(t // tm,),
            in_specs=[
                pl.BlockSpec((tm, d), lambda i, tb: (i, 0)),
                pl.BlockSpec((1, 6, d), _row_block(tm, seq)),
                pl.BlockSpec((tm, LANES), lambda i, tb: (i, 0)),
                pl.BlockSpec((1, d), lambda i, tb: (0, 0)),
                pl.BlockSpec(memory_space=pl.ANY),
            ],
            out_specs=pl.BlockSpec((tm, d), lambda i, tb: (i, 0)),
            scratch_shapes=[pltpu.VMEM((2, ns, d), BF16), pltpu.SemaphoreType.DMA((2,))],
        ),
        out_shape=jax.ShapeDtypeStruct((t, d), F32),
        compiler_params=_cparams(("arbitrary",), 40),
        name="moe_combine_final",
    )(tbl, x3, mod, info, final_g.reshape(1, d), yr)


def _moe(x3, mod, hb, info, tbl, w1, w3, w2, final_g, seq, tm):
    t, d = x3.shape
    n_experts = w1.shape[0]
    n_blk = t // tm
    ns = -(-(TOP_K * tm + n_experts * (ROW_ALIGN - 1)) // LANES) * LANES
    steps = (TOP_K * t + n_experts * n_blk * (ROW_ALIGN - 1)) // MOE_PAD + n_experts
    end = tbl[-1, 1, :n_experts] + tbl[-1, 0, :n_experts]
    nb = (end + MOE_PAD - 1) // MOE_PAD
    cum = jnp.cumsum(nb)
    n_active = cum[-1:].astype(jnp.int32)
    step = jnp.arange(steps, dtype=jnp.int32)
    owner = jnp.sum(jnp.minimum(step, n_active[0] - 1)[:, None] >= cum[None, :], axis=1)
    blk_expert = jnp.minimum(owner, n_experts - 1).astype(jnp.int32)
    flat = jnp.concatenate([tbl[:, :3, :n_experts].reshape(-1), (cum - nb) * MOE_PAD, n_active * MOE_PAD])
    flat = flat.astype(jnp.int32)
    xr = _dispatch(flat, hb, info, tm, ns, steps * MOE_PAD, n_experts)
    yr = _experts(blk_expert, n_active, xr, w1, w3, w2, w1.shape[2] // 2)
    return _combine(flat, x3, mod, info, final_g, yr, seq, tm, ns, n_experts)


def kernel(x, c, ada_w, ada_b, norm_mix_g, norm_ffn_g, final_g, ab_w_in, sgu_v_g, sgu_w_s, sgu_b_s, s5_lam_re, s5_lam_im, s5_log_dt, s5_b_re, s5_b_im, s5_c_re, s5_c_im, s5_d, s5_w_glu, ab_w_out, ffn_w1, ffn_w3, ffn_w2, mla_w_in, mla_q_norm_g, mla_kv_norm_g, mla_w_uq, mla_w_ukv, mla_w_o, moe_w_router, moe_w1, moe_w3, moe_w2):
    bn, seq, d = x.shape
    t = bn * seq
    tm = min(512, seq)
    x2 = x.reshape(t, d)
    mod = _adaln(c, ada_w, ada_b)

    a_out, u_tiles = _l0_in(x2, mod[0], norm_mix_g[0], ab_w_in[0], sgu_w_s[0], sgu_b_s[0], sgu_v_g[0], seq, tm)
    t_mat, ws, wy, aq = _s5_weights(s5_lam_re[0], s5_lam_im[0], s5_log_dt[0], s5_b_re[0], s5_b_im[0],
                                    s5_c_re[0], s5_c_im[0], S5_Q)
    y5 = _s5(u_tiles, t_mat, ws, wy, aq, bn, S5_Q)
    x2 = _l0_out(x2, mod[0], a_out, y5, u_tiles, s5_d[0], s5_w_glu[0], ab_w_out[0], seq, tm)
    x2 = _ffn(x2, mod[0], norm_ffn_g[0], ffn_w1[0], ffn_w3[0], ffn_w2[0], seq, tm, ffn_w1.shape[2] // 2)

    qt, k, vt = _mla_proj(x2, mod[1], norm_mix_g[1], mla_w_in[0], mla_q_norm_g[0], mla_kv_norm_g[0],
                          mla_w_uq[0], mla_w_ukv[0], seq, tm)
    o, w1b, w3b, w2b = _attention(qt, k, vt, bn, seq, min(1024, seq), min(1024, seq // 2),
                                  moe_w1[0], moe_w3[0], moe_w2[0])
    x3, hb, info, tbl = _attn_out(x2, mod[1], o, mla_w_o[0], norm_ffn_g[1], moe_w_router[0], seq, tm)
    out = _moe(x3, mod[1], hb, info, tbl, w1b, w3b, w2b, final_g, seq, tm)
    return out.reshape(bn, seq, d)
```

```python
import functools
import math

import jax
import jax.numpy as jnp
from jax import lax
from jax.experimental import pallas as pl
from jax.experimental.pallas import tpu as pltpu

F32 = jnp.float32
BF16 = jnp.bfloat16
HIGHEST = lax.Precision.HIGHEST

LANES = 128
SUBLANES = 8
MIB = 1024 * 1024

RMS_EPS = 1e-6
C_HEADS = 8
QK_NOPE = 128
QK_ROPE = 64
V_HEAD = 128
ROPE_THETA = 10000.0
TOP_K = 2

S5_Q = 8
MOE_PAD = 512


def _cparams(semantics, vmem_mib):
    return pltpu.CompilerParams(dimension_semantics=semantics, vmem_limit_bytes=vmem_mib * MIB)


def _dot(a, b):
    return jnp.dot(a, b, preferred_element_type=F32)


def _norm_mod(x, g, shift, scale):
    y = x * lax.rsqrt(jnp.mean(x * x, axis=-1, keepdims=True) + RMS_EPS)
    return (y * g) * (1.0 + scale) + shift


def _row_block(tm, seq):
    per_batch = seq // tm
    return lambda i, *_: (i // per_batch, 0, 0)


def _adaln_kernel(c_ref, w_ref, b_ref, o_ref):
    c = c_ref[...]
    ca = c * jax.nn.sigmoid(c)
    o_ref[0] = jnp.dot(ca, w_ref[0], preferred_element_type=F32, precision=HIGHEST) + b_ref[0]


def _adaln(c, ada_w, ada_b):
    depth, d, n = ada_w.shape
    bn = c.shape[0]
    tn = n // 4
    c8 = jnp.zeros((SUBLANES, d), F32).at[:bn].set(c)
    out = pl.pallas_call(
        _adaln_kernel,
        grid=(depth, n // tn),
        in_specs=[
            pl.BlockSpec((SUBLANES, d), lambda l, j: (0, 0)),
            pl.BlockSpec((1, d, tn), lambda l, j: (l, 0, j)),
            pl.BlockSpec((1, 1, tn), lambda l, j: (l, 0, j)),
        ],
        out_specs=pl.BlockSpec((1, SUBLANES, tn), lambda l, j: (l, 0, j)),
        out_shape=jax.ShapeDtypeStruct((depth, SUBLANES, n), F32),
        compiler_params=_cparams(("parallel", "parallel"), 32),
        name="adaln",
    )(c8, ada_w, ada_b.reshape(depth, 1, n))
    return out[:, :bn].reshape(depth, bn, 6, d)


def _l0_in_kernel(x_ref, mod_ref, g_ref, win_ref, ws_ref, bs_ref, vg_ref, a_ref, u_ref, *,
                  heads, chunk, n_tiles):
    x = x_ref[...]
    h = _norm_mod(x, g_ref[...], mod_ref[0, 0:1, :], mod_ref[0, 1:2, :]).astype(BF16)
    z = _dot(h, win_ref[...])
    tm = x.shape[0]
    a_width = heads * LANES
    for g in range(heads):
        u = jax.nn.gelu(z[:, g * LANES:(g + 1) * LANES])
        v = jax.nn.gelu(z[:, a_width + g * LANES:a_width + (g + 1) * LANES])
        vn = v * lax.rsqrt(jnp.mean(v * v, axis=-1, keepdims=True) + RMS_EPS) * vg_ref[g:g + 1, :]
        vnb = vn.astype(BF16)
        bias = bs_ref[:, g:g + 1]
        for ci in range(tm // chunk):
            rows = slice(ci * chunk, (ci + 1) * chunk)
            s = _dot(ws_ref[g], vnb[rows]) + bias
            a_ref[rows, g * LANES:(g + 1) * LANES] = (u[rows] * s).astype(BF16)
    for j in range(n_tiles):
        u_ref[j] = z[:, 2 * a_width + j * LANES:2 * a_width + (j + 1) * LANES].astype(BF16)


def _l0_in(x2, mod, norm_g, w_in, w_s, b_s, v_g, seq, tm):
    t, d = x2.shape
    heads, chunk, _ = w_s.shape
    a_width = heads * LANES
    b_width = w_in.shape[1] - 2 * a_width
    n_tiles = b_width // LANES
    kern = functools.partial(_l0_in_kernel, heads=heads, chunk=chunk, n_tiles=n_tiles)
    return pl.pallas_call(
        kern,
        grid=(t // tm,),
        in_specs=[
            pl.BlockSpec((tm, d), lambda i: (i, 0)),
            pl.BlockSpec((1, 6, d), _row_block(tm, seq)),
            pl.BlockSpec((1, d), lambda i: (0, 0)),
            pl.BlockSpec(w_in.shape, lambda i: (0, 0)),
            pl.BlockSpec(w_s.shape, lambda i: (0, 0, 0)),
            pl.BlockSpec((chunk, heads), lambda i: (0, 0)),
            pl.BlockSpec((heads, LANES), lambda i: (0, 0)),
        ],
        out_specs=[
            pl.BlockSpec((tm, a_width), lambda i: (i, 0)),
            pl.BlockSpec((n_tiles, tm, LANES), lambda i: (0, i, 0)),
        ],
        out_shape=[
            jax.ShapeDtypeStruct((t, a_width), BF16),
            jax.ShapeDtypeStruct((n_tiles, t, LANES), BF16),
        ],
        compiler_params=_cparams(("parallel",), 40),
        name="l0_in_sgu",
    )(x2, mod, norm_g.reshape(1, d), w_in.astype(BF16), w_s.astype(BF16), b_s.T, v_g)


def _s5_weights(lam_re, lam_im, log_dt, b_re, b_im, c_re, c_im, q):
    _, groups, p = lam_re.shape
    h = b_re.shape[-1]
    gpt = LANES // h
    nt = groups // gpt
    dt = jnp.exp(log_dt.astype(F32))[..., None]
    mag = jnp.exp(lam_re * dt)
    ang = lam_im * dt
    ar, ai = mag * jnp.cos(ang), mag * jnp.sin(ang)
    den = lam_re * lam_re + lam_im * lam_im
    fr = ((ar - 1.0) * lam_re + ai * lam_im) / den
    fi = (ai * lam_re - (ar - 1.0) * lam_im) / den
    bbr = fr[..., None] * b_re - fi[..., None] * b_im
    bbi = fr[..., None] * b_im + fi[..., None] * b_re
    prs, pis = [jnp.ones_like(ar)], [jnp.zeros_like(ar)]
    for _ in range(q):
        prs.append(prs[-1] * ar - pis[-1] * ai)
        pis.append(prs[-2] * ai + pis[-1] * ar)
    pr, pi = jnp.stack(prs), jnp.stack(pis)
    abr = pr[..., None] * bbr - pi[..., None] * bbi
    abi = pr[..., None] * bbi + pi[..., None] * bbr
    car = c_re * pr[:, :, :, None, :] - c_im * pi[:, :, :, None, :]
    cai = c_re * pi[:, :, :, None, :] + c_im * pr[:, :, :, None, :]
    kk = (jnp.einsum('dgop,kdgpi->kdgoi', c_re, abr[:q], precision=HIGHEST)
          - jnp.einsum('dgop,kdgpi->kdgoi', c_im, abi[:q], precision=HIGHEST))
    idx = jnp.arange(q)
    dfi = idx[None, :] - idx[:, None]
    kf = jnp.where((dfi >= 0)[:, :, None, None, None], kk[:, 0][jnp.clip(dfi, 0, q - 1)], 0.0)
    kb = jnp.where((dfi <= 0)[:, :, None, None, None], kk[:, 1][jnp.clip(-dfi, 0, q - 1)], 0.0)
    m = (kf + kb).reshape(q, q, nt, gpt, h, h)
    w4 = jnp.stack([jnp.stack([abr[q - 1 - idx, 0], abi[q - 1 - idx, 0]]),
                    jnp.stack([abr[idx, 1], abi[idx, 1]])])
    w4 = w4.reshape(2, 2, q, nt, gpt, p, h)
    y4 = jnp.stack([jnp.stack([car[idx + 1, 0], -cai[idx + 1, 0]]),
                    jnp.stack([car[q - idx, 1], -cai[q - idx, 1]])])
    y4 = y4.reshape(2, 2, q, nt, gpt, h, p)

    a_in = m.transpose(2, 0, 3, 5, 1, 4).reshape(nt, q * LANES, q * h)
    a_ws = w4.transpose(3, 2, 4, 6, 0, 1, 5).reshape(nt, q * LANES, 4 * p)
    a_wy = y4.transpose(3, 0, 1, 4, 6, 2, 5).reshape(nt, 4 * gpt * p, q * h)

    def expand(a, col_src, row_group, col_group):
        pick = (jnp.arange(a.shape[2])[:, None] == col_src[None, :]).astype(BF16)
        wide = jnp.einsum('jrk,kc->jrc', a.astype(BF16), pick, preferred_element_type=F32)
        return jnp.where(row_group[:, None] == col_group[None, :], wide, 0.0).astype(BF16)

    r_in = jnp.arange(q * LANES)
    r_st = jnp.arange(4 * gpt * p)
    g_in = (r_in // h) % gpt
    g_st = (r_st // p) % gpt
    src_in = (r_in // LANES) * h + r_in % h
    src_st = (r_st // (gpt * p)) * p + r_st % p
    t_mat = expand(a_in, src_in, g_in, g_in)
    ws = expand(a_ws, src_st, g_in, g_st)
    wy = expand(a_wy, src_in, g_st, g_in)

    aq = jnp.stack([jnp.stack([pr[q, 0], pi[q, 0]]), jnp.stack([pr[q, 1], pi[q, 1]])])
    aq = aq.reshape(2, 2, nt, gpt, p).transpose(2, 0, 1, 3, 4).reshape(nt, 1, 4 * gpt * p)
    return t_mat, ws, wy, aq


def _s5_kernel(u_ref, t_ref, ws_ref, wy_ref, aq_ref, y_ref, s_ref):
    u = u_ref[0]
    s_ref[...] = _dot(u, ws_ref[0])
    rows, width = s_ref.shape
    hw = width // 4
    aq = aq_ref[0]
    afr, afi = aq[:, 0:hw], aq[:, hw:2 * hw]
    abr, abi = aq[:, 2 * hw:3 * hw], aq[:, 3 * hw:4 * hw]

    def body(c, carry):
        xfr, xfi, xbr, xbi = carry
        rf = pl.ds(c, 1)
        rb = pl.ds(rows - 1 - c, 1)
        sfr = s_ref[rf, 0:hw]
        sfi = s_ref[rf, hw:2 * hw]
        sbr = s_ref[rb, 2 * hw:3 * hw]
        sbi = s_ref[rb, 3 * hw:4 * hw]
        s_ref[rf, 0:hw] = xfr
        s_ref[rf, hw:2 * hw] = xfi
        s_ref[rb, 2 * hw:3 * hw] = xbr
        s_ref[rb, 3 * hw:4 * hw] = xbi
        return (afr * xfr - afi * xfi + sfr, afr * xfi + afi * xfr + sfi,
                abr * xbr - abi * xbi + sbr, abr * xbi + abi * xbr + sbi)

    zero = jnp.zeros((1, hw), F32)
    lax.fori_loop(0, rows, body, (zero, zero, zero, zero), unroll=8)
    y_ref[0] = _dot(u, t_ref[0]) + _dot(s_ref[...].astype(BF16), wy_ref[0])


def _s5(u_tiles, t_mat, ws, wy, aq, bn, q):
    nt, t, _ = u_tiles.shape
    rows = t // q // bn
    u2 = u_tiles.reshape(nt, t // q, q * LANES)
    sw = ws.shape[2]
    y2 = pl.pallas_call(
        _s5_kernel,
        grid=(nt, bn),
        in_specs=[
            pl.BlockSpec((1, rows, q * LANES), lambda j, b: (j, b, 0)),
            pl.BlockSpec((1,) + t_mat.shape[1:], lambda j, b: (j, 0, 0)),
            pl.BlockSpec((1,) + ws.shape[1:], lambda j, b: (j, 0, 0)),
            pl.BlockSpec((1,) + wy.shape[1:], lambda j, b: (j, 0, 0)),
            pl.BlockSpec((1, 1, sw), lambda j, b: (j, 0, 0)),
        ],
        out_specs=pl.BlockSpec((1, rows, q * LANES), lambda j, b: (j, b, 0)),
        out_shape=jax.ShapeDtypeStruct((nt, t // q, q * LANES), F32),
        scratch_shapes=[pltpu.VMEM((rows, sw), F32)],
        compiler_params=_cparams(("parallel", "parallel"), 48),
        name="s5_chunked",
    )(u2, t_mat, ws, wy, aq)
    return y2.reshape(nt, t, LANES)


def _l0_out_kernel(x_ref, mod_ref, a_ref, y5_ref, u_ref, d_ref, wglu_ref, wout_ref, o_ref, *, n_tiles):
    ys = []
    for j in range(n_tiles):
        ys.append(jax.nn.gelu(y5_ref[j] + d_ref[j] * u_ref[j].astype(F32)))
    y = jnp.concatenate(ys, axis=-1)
    glu = jax.nn.sigmoid(_dot(y.astype(BF16), wglu_ref[...]))
    b_out = (y * glu).astype(BF16)
    a_width = a_ref.shape[1]
    out = _dot(a_ref[...], wout_ref[0:a_width, :]) + _dot(b_out, wout_ref[a_width:, :])
    o_ref[...] = x_ref[...] + mod_ref[0, 2:3, :] * out


def _l0_out(x2, mod, a_out, y5, u_tiles, d_skip, w_glu, w_out, seq, tm):
    t, d = x2.shape
    n_tiles = u_tiles.shape[0]
    a_width = a_out.shape[1]
    kern = functools.partial(_l0_out_kernel, n_tiles=n_tiles)
    return pl.pallas_call(
        kern,
        grid=(t // tm,),
        in_specs=[
            pl.BlockSpec((tm, d), lambda i: (i, 0)),
            pl.BlockSpec((1, 6, d), _row_block(tm, seq)),
            pl.BlockSpec((tm, a_width), lambda i: (i, 0)),
            pl.BlockSpec((n_tiles, tm, LANES), lambda i: (0, i, 0)),
            pl.BlockSpec((n_tiles, tm, LANES), lambda i: (0, i, 0)),
            pl.BlockSpec((n_tiles, 1, LANES), lambda i: (0, 0, 0)),
            pl.BlockSpec(w_glu.shape, lambda i: (0, 0)),
            pl.BlockSpec(w_out.shape, lambda i: (0, 0)),
        ],
        out_specs=pl.BlockSpec((tm, d), lambda i: (i, 0)),
        out_shape=jax.ShapeDtypeStruct((t, d), F32),
        compiler_params=_cparams(("parallel",), 40),
        name="l0_out",
    )(x2, mod, a_out, y5, u_tiles, d_skip.reshape(n_tiles, 1, LANES),
      w_glu.astype(BF16), w_out.astype(BF16))


def _ffn_kernel(x_ref, mod_ref, g_ref, w1_ref, w3_ref, w2_ref, o_ref):
    x = x_ref[...]
    h = _norm_mod(x, g_ref[...], mod_ref[0, 3:4, :], mod_ref[0, 4:5, :]).astype(BF16)
    a = _dot(h, w1_ref[...])
    b = _dot(h, w3_ref[...])
    act = (a * jax.nn.sigmoid(a) * b).astype(BF16)
    o_ref[...] = x + mod_ref[0, 5:6, :] * _dot(act, w2_ref[...])


def _ffn(x2, mod, norm_g, w1, w3, w2, seq, tm):
    t, d = x2.shape
    resident = lambda shape: pl.BlockSpec(shape, lambda i: (0, 0), pipeline_mode=pl.Buffered(1))
    return pl.pallas_call(
        _ffn_kernel,
        grid=(t // tm,),
        in_specs=[
            pl.BlockSpec((tm, d), lambda i: (i, 0)),
            pl.BlockSpec((1, 6, d), _row_block(tm, seq)),
            pl.BlockSpec((1, d), lambda i: (0, 0)),
            resident(w1.shape),
            resident(w3.shape),
            resident(w2.shape),
        ],
        out_specs=pl.BlockSpec((tm, d), lambda i: (i, 0)),
        out_shape=jax.ShapeDtypeStruct((t, d), F32),
        compiler_params=_cparams(("parallel",), 48),
        name="ffn_swiglu",
    )(x2, mod, norm_g.reshape(1, d), w1.astype(BF16), w3.astype(BF16), w2.astype(BF16))


_NT_DIMS = (((1,), (1,)), ((), ()))
_TN_DIMS = (((0,), (0,)), ((), ()))


def _mla_proj_kernel(x_ref, mod_ref, g_ref, win_ref, gq_ref, gkv_ref, wqt_ref, wkn_ref, wvt_ref,
                     cos_ref, sin_ref, cost_ref, sint_ref, qt_ref, k_ref, vt_ref, *, q_lora, kv_lora, heads):
    h = _norm_mod(x_ref[...], g_ref[...], mod_ref[0, 0:1, :], mod_ref[0, 1:2, :]).astype(BF16)
    z = _dot(h, win_ref[...])
    cq = z[:, :q_lora]
    cq = (cq * lax.rsqrt(jnp.mean(cq * cq, axis=-1, keepdims=True) + RMS_EPS) * gq_ref[...]).astype(BF16)
    ckv = z[:, q_lora:q_lora + kv_lora]
    ckv = (ckv * lax.rsqrt(jnp.mean(ckv * ckv, axis=-1, keepdims=True) + RMS_EPS) * gkv_ref[...]).astype(BF16)
    r0 = q_lora + kv_lora
    k_rope = (z[:, r0:r0 + LANES] * cos_ref[...] + z[:, r0 + LANES:r0 + 2 * LANES] * sin_ref[...]).astype(BF16)
    kn = _dot(ckv, wkn_ref[...])
    qat = lax.dot_general(wqt_ref[...], cq, _NT_DIMS, preferred_element_type=F32)
    vt_ref[0] = lax.dot_general(wvt_ref[...], ckv, _NT_DIMS, preferred_element_type=F32).astype(BF16)
    cost = cost_ref[...]
    sint = sint_ref[...]
    hw = heads * LANES
    for hd in range(heads):
        c = slice(hd * LANES, (hd + 1) * LANES)
        qt_ref[2 * hd * LANES:(2 * hd + 1) * LANES, :] = qat[c, :].astype(BF16)
        q_rope = qat[hw + hd * LANES:hw + (hd + 1) * LANES, :] * cost \
            + qat[2 * hw + hd * LANES:2 * hw + (hd + 1) * LANES, :] * sint
        qt_ref[(2 * hd + 1) * LANES:(2 * hd + 2) * LANES, :] = q_rope.astype(BF16)
        k_ref[:, 2 * hd * LANES:(2 * hd + 1) * LANES] = kn[:, c].astype(BF16)
        k_ref[:, (2 * hd + 1) * LANES:(2 * hd + 2) * LANES] = k_rope


def _rope_pad(w_rope):
    half = QK_ROPE // 2
    x1, x2 = w_rope[..., :half], w_rope[..., half:]
    zeros = jnp.zeros(w_rope.shape[:-1] + (LANES - QK_ROPE,), w_rope.dtype)
    return jnp.concatenate([x1, x2, zeros], -1), jnp.concatenate([-x2, x1, zeros], -1)


def _mla_proj(x2, mod, norm_g, w_in, gq, gkv, w_uq, w_ukv, seq, tm):
    t, d = x2.shape
    q_lora, kv_lora = gq.shape[0], gkv.shape[0]
    heads = C_HEADS
    scale = math.log2(math.e) / math.sqrt(QK_NOPE + QK_ROPE)
    kr, krs = _rope_pad(w_in[:, q_lora + kv_lora:])
    w_in_ext = jnp.concatenate([w_in[:, :q_lora + kv_lora], kr, krs], -1).astype(BF16)
    wq3 = (w_uq * scale).reshape(q_lora, heads, QK_NOPE + QK_ROPE)
    qr, qrs = _rope_pad(wq3[..., QK_NOPE:])
    wqt = jnp.concatenate([wq3[..., :QK_NOPE].reshape(q_lora, -1), qr.reshape(q_lora, -1),
                           qrs.reshape(q_lora, -1)], -1).T.astype(BF16)
    wkv3 = w_ukv.reshape(kv_lora, heads, QK_NOPE + V_HEAD)
    wkn = wkv3[..., :QK_NOPE].reshape(kv_lora, -1).astype(BF16)
    wvt = wkv3[..., QK_NOPE:].reshape(kv_lora, -1).T.astype(BF16)
    inv_freq = ROPE_THETA ** (-jnp.arange(0, QK_ROPE, 2, dtype=F32) / QK_ROPE)
    ang = jnp.arange(seq, dtype=F32)[:, None] * inv_freq[None, :]
    pad = jnp.zeros((seq, LANES - QK_ROPE), F32)
    cos = jnp.concatenate([jnp.cos(ang), jnp.cos(ang), pad], -1)
    sin = jnp.concatenate([jnp.sin(ang), jnp.sin(ang), pad], -1)
    per_batch = seq // tm
    kern = functools.partial(_mla_proj_kernel, q_lora=q_lora, kv_lora=kv_lora, heads=heads)
    const = lambda i: (0, 0)
    return pl.pallas_call(
        kern,
        grid=(t // tm,),
        in_specs=[
            pl.BlockSpec((tm, d), lambda i: (i, 0)),
            pl.BlockSpec((1, 6, d), _row_block(tm, seq)),
            pl.BlockSpec((1, d), const),
            pl.BlockSpec(w_in_ext.shape, const),
            pl.BlockSpec((1, q_lora), const),
            pl.BlockSpec((1, kv_lora), const),
            pl.BlockSpec(wqt.shape, const),
            pl.BlockSpec(wkn.shape, const),
            pl.BlockSpec(wvt.shape, const),
            pl.BlockSpec((tm, LANES), lambda i: (i % per_batch, 0)),
            pl.BlockSpec((tm, LANES), lambda i: (i % per_batch, 0)),
            pl.BlockSpec((LANES, tm), lambda i: (0, i % per_batch)),
            pl.BlockSpec((LANES, tm), lambda i: (0, i % per_batch)),
        ],
        out_specs=[
            pl.BlockSpec((2 * heads * LANES, tm), lambda i: (0, i)),
            pl.BlockSpec((tm, 2 * heads * LANES), lambda i: (i, 0)),
            pl.BlockSpec((1, heads * V_HEAD, tm), lambda i: (i, 0, 0)),
        ],
        out_shape=[
            jax.ShapeDtypeStruct((2 * heads * LANES, t), BF16),
            jax.ShapeDtypeStruct((t, 2 * heads * LANES), BF16),
            jax.ShapeDtypeStruct((t // tm, heads * V_HEAD, tm), BF16),
        ],
        compiler_params=_cparams(("parallel",), 48),
        name="mla_proj",
    )(x2, mod, norm_g.reshape(1, d), w_in_ext, gq.reshape(1, -1), gkv.reshape(1, -1), wqt, wkn, wvt,
      cos, sin, cos.T, sin.T)


def _attn_kernel(qt_ref, k_ref, vt_ref, w1_ref, w3_ref, w2_ref, o_ref, w1b_ref, w3b_ref, w2b_ref,
                 s_scr, acc_scr, *, tk, per_trip):
    w1b_ref[...] = w1_ref[...].astype(BF16)
    w3b_ref[...] = w3_ref[...].astype(BF16)
    w2b_ref[...] = w2_ref[...].astype(BF16)

    tq = qt_ref.shape[1]
    n_kv = k_ref.shape[0] // tk
    tv = vt_ref.shape[2]
    sub = tk // tv

    def scores(c, slot):
        k = k_ref[pl.ds(pl.multiple_of(c * tk, tk), tk), :]
        s_scr[slot] = _dot(k, qt_ref[...])

    def update(c, slot, m, l):
        s = s_scr[slot]
        m_new = jnp.maximum(m, jnp.max(s, axis=0, keepdims=True))
        alpha = jnp.exp2(m - m_new)
        p = jnp.exp2(s - m_new)
        l = alpha * l + jnp.sum(p, axis=0, keepdims=True)
        pb = p.astype(BF16)
        acc = alpha * acc_scr[...]
        for j in range(sub):
            acc = acc + _dot(vt_ref[c * sub + j], pb[j * tv:(j + 1) * tv])
        acc_scr[...] = acc
        return m_new, l

    def steps(c0, count, m, l, final):
        for u in range(count):
            if not (final and u == count - 1):
                scores(c0 + u + 1, (u + 1) % 2)
            m, l = update(c0 + u, u % 2, m, l)
        return m, l

    def body(trip, carry):
        return steps(trip * per_trip, per_trip, *carry, final=False)

    acc_scr[...] = jnp.zeros_like(acc_scr)
    scores(0, 0)
    trips = n_kv // per_trip - 1
    m, l = lax.fori_loop(0, trips, body, (jnp.full((1, tq), -jnp.inf, F32), jnp.zeros((1, tq), F32)))
    m, l = steps(trips * per_trip, per_trip, m, l, final=True)
    o_ref[0] = (acc_scr[...] / l).astype(BF16)


def _attention(qt, k, vt, bn, seq, tq, tk, w1, w3, w2):
    tv = vt.shape[2]
    t = k.shape[0]
    heads = C_HEADS
    nq = seq // tq
    n_steps = bn * heads * nq
    kern = functools.partial(_attn_kernel, tk=tk, per_trip=2)

    def step(b, h, i):
        return ((b * heads + h) * nq + i, 0)

    flat = [w.reshape(-1, w.shape[-1]) for w in (w1, w3, w2)]
    w_specs = [pl.BlockSpec((w.shape[0] // n_steps, w.shape[1]), step) for w in flat]
    for w in flat:
        assert w.shape[0] % (n_steps * ROW_ALIGN) == 0, "weight rows must split evenly over the grid steps"
    o, w1b, w3b, w2b = pl.pallas_call(
        kern,
        grid=(bn, heads, nq),
        in_specs=[
            pl.BlockSpec((2 * LANES, tq), lambda b, h, i: (h, b * nq + i)),
            pl.BlockSpec((seq, 2 * LANES), lambda b, h, i: (b, h)),
            pl.BlockSpec((seq // tv, V_HEAD, tv), lambda b, h, i: (b, h, 0)),
        ] + w_specs,
        out_specs=[pl.BlockSpec((1, V_HEAD, tq), lambda b, h, i: (b * nq + i, h, 0))] + w_specs,
        out_shape=[jax.ShapeDtypeStruct((t // tq, heads * V_HEAD, tq), BF16)]
        + [jax.ShapeDtypeStruct(w.shape, BF16) for w in flat],
        scratch_shapes=[pltpu.VMEM((2, tk, tq), F32), pltpu.VMEM((V_HEAD, tq), F32)],
        compiler_params=_cparams(("parallel", "parallel", "parallel"), 48),
        name="mla_attention",
    )(qt, k, vt, *flat)
    return o, w1b.reshape(w1.shape), w3b.reshape(w3.shape), w2b.reshape(w2.shape)


ROW_ALIGN = 16
SLAB_SIZES = (512, 256, 128, 64, 32, 16)


def _slab_copies(src_ref, src_row, dst_ref, dst_row, n_rows, sem, fn):
    done = 0
    for size in SLAB_SIZES:
        take = (n_rows & size) != 0
        s0 = pl.multiple_of(src_row + done, ROW_ALIGN)
        d0 = pl.multiple_of(dst_row + done, ROW_ALIGN)

        @pl.when(take)
        def _(s0=s0, d0=d0, size=size):
            fn(pltpu.make_async_copy(src_ref.at[pl.ds(s0, size), :], dst_ref.at[pl.ds(d0, size), :], sem))

        done = done + jnp.where(take, size, 0)


def _attn_out_kernel(x_ref, mod_ref, o_ref, wo_ref, g_ref, wr_ref, x3_ref, h_ref, info_ref, tbl_ref,
                     run_scr, *, n_experts):
    @pl.when(pl.program_id(0) == 0)
    def _():
        run_scr[...] = jnp.zeros_like(run_scr)

    y = lax.dot_general(o_ref[0], wo_ref[...], _TN_DIMS, preferred_element_type=F32)
    x3 = x_ref[...] + mod_ref[0, 2:3, :] * y
    x3_ref[...] = x3
    h = _norm_mod(x3, g_ref[...], mod_ref[0, 3:4, :], mod_ref[0, 4:5, :])
    h_ref[...] = h.astype(BF16)
    tm, d = h.shape

    h_hi = h.astype(BF16)
    h_lo = (h - h_hi.astype(F32)).astype(BF16)
    wr = wr_ref[...]
    w_hi = wr.astype(BF16)
    w_lo = (wr - w_hi.astype(F32)).astype(BF16)
    logits = _dot(h_hi, w_hi) + (_dot(h_hi, w_lo) + _dot(h_lo, w_hi))
    lane = lax.broadcasted_iota(jnp.int32, (tm, LANES), 1).astype(F32)
    neg = jnp.float32(-jnp.inf)
    lg = jnp.where(lane < n_experts, logits, neg)
    m1 = jnp.max(lg, axis=-1, keepdims=True)
    i1 = jnp.min(jnp.where(lg == m1, lane, float(LANES)), axis=-1, keepdims=True)
    lg2 = jnp.where(lane == i1, neg, lg)
    m2 = jnp.max(lg2, axis=-1, keepdims=True)
    i2 = jnp.min(jnp.where(lg2 == m2, lane, float(LANES)), axis=-1, keepdims=True)
    e = jnp.exp(m2 - m1)
    g0 = 1.0 / (1.0 + e)
    g1 = e / (1.0 + e)
    sel1 = lane == i1
    sel2 = lane == i2
    onehot = jnp.where(sel1, 1.0, 0.0) + jnp.where(sel2, 1.0, 0.0)
    row = lax.broadcasted_iota(jnp.int32, (tm, tm), 0)
    col = lax.broadcasted_iota(jnp.int32, (tm, tm), 1)
    tri = jnp.where(col < row, 1.0, 0.0).astype(BF16)
    before = _dot(tri, onehot.astype(BF16))
    n_pick = jnp.sum(onehot, axis=0, keepdims=True)
    n_slab = jnp.floor((n_pick + (ROW_ALIGN - 1)) * (1.0 / ROW_ALIGN)) * ROW_ALIGN
    lrow = lax.broadcasted_iota(jnp.int32, (LANES, LANES), 0)
    lcol = lax.broadcasted_iota(jnp.int32, (LANES, LANES), 1)
    upper = jnp.where(lrow < lcol, 1.0, 0.0).astype(BF16)
    n8 = jnp.broadcast_to(n_slab, (SUBLANES, LANES))
    loc = _dot(n8.astype(BF16), upper)[0:1]
    where = before + loc
    pos0 = jnp.sum(jnp.where(sel1, where, 0.0), axis=-1, keepdims=True)
    pos1 = jnp.sum(jnp.where(sel2, where, 0.0), axis=-1, keepdims=True)
    info_ref[...] = jnp.where(lane == 0, pos0, jnp.where(lane == 1, pos1, jnp.where(
        lane == 2, g0, jnp.where(lane == 3, g1, 0.0))))

    off = run_scr[...]
    run_scr[...] = off + n_slab
    trow = lax.broadcasted_iota(jnp.int32, (SUBLANES, LANES), 0)
    tbl = jnp.where(trow == 0, n_slab, jnp.where(trow == 1, off, jnp.where(trow == 2, loc, 0.0)))
    tbl_ref[0] = tbl.astype(jnp.int32)


def _attn_out(x2, mod, o, w_o, norm_g, w_router, seq, tm):
    t, d = x2.shape
    n_experts = w_router.shape[1]
    wr = jnp.zeros((d, LANES), F32).at[:, :n_experts].set(w_router)
    kern = functools.partial(_attn_out_kernel, n_experts=n_experts)
    o_split = o.shape[2] // tm
    const = lambda i: (0, 0)
    return pl.pallas_call(
        kern,
        grid=(t // tm,),
        in_specs=[
            pl.BlockSpec((tm, d), lambda i: (i, 0)),
            pl.BlockSpec((1, 6, d), _row_block(tm, seq)),
            pl.BlockSpec((1, o.shape[1], tm), lambda i: (i // o_split, 0, i % o_split)),
            pl.BlockSpec(w_o.shape, const),
            pl.BlockSpec((1, d), const),
            pl.BlockSpec((d, LANES), const),
        ],
        out_specs=[
            pl.BlockSpec((tm, d), lambda i: (i, 0)),
            pl.BlockSpec((tm, d), lambda i: (i, 0)),
            pl.BlockSpec((tm, LANES), lambda i: (i, 0)),
            pl.BlockSpec((1, SUBLANES, LANES), lambda i: (i, 0, 0)),
        ],
        out_shape=[
            jax.ShapeDtypeStruct((t, d), F32),
            jax.ShapeDtypeStruct((t, d), BF16),
            jax.ShapeDtypeStruct((t, LANES), F32),
            jax.ShapeDtypeStruct((t // tm, SUBLANES, LANES), jnp.int32),
        ],
        scratch_shapes=[pltpu.VMEM((1, LANES), F32)],
        compiler_params=_cparams(("arbitrary",), 40),
        name="attn_out_router",
    )(x2, mod, o, w_o.astype(BF16), norm_g.reshape(1, d), wr)


def _dispatch_kernel(tbl_ref, h_ref, info_ref, xr_hbm, xs_scr, zero_scr, sem, *, n_experts, n_fill):
    i = pl.program_id(0)
    n_blk = pl.num_programs(0)
    tm = h_ref.shape[0]
    ns = xs_scr.shape[0]
    base = i * (3 * n_experts)
    starts = n_blk * (3 * n_experts)

    srow = lax.broadcasted_iota(jnp.int32, (tm, ns), 1).astype(F32)
    pick = jnp.logical_or(srow == info_ref[:, 0:1], srow == info_ref[:, 1:2])
    pick = jnp.where(pick, 1.0, 0.0).astype(BF16)
    xs_scr[...] = lax.dot_general(pick, h_ref[...], _TN_DIMS, preferred_element_type=F32).astype(BF16)

    def slabs(fn):
        for e in range(n_experts):
            dst = tbl_ref[starts + e] + tbl_ref[base + n_experts + e]
            _slab_copies(xs_scr, tbl_ref[base + 2 * n_experts + e], xr_hbm, dst, tbl_ref[base + e], sem, fn)

    slabs(lambda cp: cp.start())
    slabs(lambda cp: cp.wait())

    @pl.when(i == n_blk - 1)
    def _():
        zero_scr[...] = jnp.zeros_like(zero_scr)
        used = tbl_ref[starts + n_experts]

        def fills(fn):
            for e in range(n_experts):
                end = tbl_ref[starts + e] + tbl_ref[base + n_experts + e] + tbl_ref[base + e]
                _slab_copies(zero_scr, 0, xr_hbm, end, (-end) & (MOE_PAD - 1), sem, fn)
            for j in range(n_fill):
                row = pl.multiple_of(used + j * MOE_PAD, MOE_PAD)

                @pl.when(row < xr_hbm.shape[0])
                def _(row=row):
                    fn(pltpu.make_async_copy(zero_scr, xr_hbm.at[pl.ds(row, MOE_PAD), :], sem))

        fills(lambda cp: cp.start())
        fills(lambda cp: cp.wait())


def _dispatch(tbl, hb, info, tm, ns, n_rows, n_experts):
    t, d = hb.shape
    n_fill = n_rows // MOE_PAD - (TOP_K * t) // MOE_PAD
    kern = functools.partial(_dispatch_kernel, n_experts=n_experts, n_fill=n_fill)
    return pl.pallas_call(
        kern,
        grid_spec=pltpu.PrefetchScalarGridSpec(
            num_scalar_prefetch=1,
            grid=(t // tm,),
            in_specs=[
                pl.BlockSpec((tm, d), lambda i, tb: (i, 0)),
                pl.BlockSpec((tm, LANES), lambda i, tb: (i, 0)),
            ],
            out_specs=pl.BlockSpec(memory_space=pl.ANY),
            scratch_shapes=[pltpu.VMEM((ns, d), BF16), pltpu.VMEM((MOE_PAD, d), BF16),
                            pltpu.SemaphoreType.DMA(())],
        ),
        out_shape=jax.ShapeDtypeStruct((n_rows, d), BF16),
        compiler_params=_cparams(("arbitrary",), 40),
        name="moe_dispatch",
    )(tbl, hb, info)


def _experts_kernel(be_ref, na_ref, x_ref, w1_ref, w3_ref, w2_ref, o_ref, acc_scr):
    b = pl.program_id(0)
    f = pl.program_id(1)
    active = b < na_ref[0]
    last = f == pl.num_programs(1) - 1

    @pl.when(jnp.logical_and(active, f == 0))
    def _():
        acc_scr[...] = jnp.zeros_like(acc_scr)

    @pl.when(active)
    def _():
        x = x_ref[...]
        a = _dot(x, w1_ref[0])
        g = _dot(x, w3_ref[0])
        act = (a * jax.nn.sigmoid(a) * g).astype(BF16)
        acc_scr[...] += _dot(act, w2_ref[0])

    @pl.when(jnp.logical_and(active, last))
    def _():
        o_ref[...] = acc_scr[...].astype(BF16)

    @pl.when(jnp.logical_and(jnp.logical_not(active), last))
    def _():
        o_ref[...] = jnp.zeros_like(o_ref)


def _experts(blk_expert, n_active, xr, w1, w3, w2, tf):
    rows = MOE_PAD
    _, d, ff = w1.shape
    last_f = ff // tf - 1

    def ftile(b, f, na):
        return jnp.where(b < na[0], f, last_f)

    return pl.pallas_call(
        _experts_kernel,
        grid_spec=pltpu.PrefetchScalarGridSpec(
            num_scalar_prefetch=2,
            grid=(xr.shape[0] // rows, ff // tf),
            in_specs=[
                pl.BlockSpec((rows, d), lambda b, f, be, na: (b, 0)),
                pl.BlockSpec((1, d, tf), lambda b, f, be, na: (be[b], 0, ftile(b, f, na))),
                pl.BlockSpec((1, d, tf), lambda b, f, be, na: (be[b], 0, ftile(b, f, na))),
                pl.BlockSpec((1, tf, d), lambda b, f, be, na: (be[b], ftile(b, f, na), 0)),
            ],
            out_specs=pl.BlockSpec((rows, d), lambda b, f, be, na: (b, 0)),
            scratch_shapes=[pltpu.VMEM((rows, d), F32)],
        ),
        out_shape=jax.ShapeDtypeStruct(xr.shape, BF16),
        compiler_params=_cparams(("parallel", "arbitrary"), 48),
        name="moe_experts",
    )(blk_expert, n_active, xr, w1, w3, w2)


def _combine_kernel(tbl_ref, x_ref, mod_ref, info_ref, fg_ref, y_hbm, o_ref, ys_scr, sem, *, n_experts):
    tm, d = x_ref.shape
    ns = ys_scr.shape[1]
    i = pl.program_id(0)
    n_blk = pl.num_programs(0)
    starts = n_blk * (3 * n_experts)

    def slabs(blk, fn):
        slot = blk % 2
        base = blk * (3 * n_experts)
        for e in range(n_experts):
            src = tbl_ref[starts + e] + tbl_ref[base + n_experts + e]
            _slab_copies(y_hbm, src, ys_scr.at[slot], tbl_ref[base + 2 * n_experts + e], tbl_ref[base + e],
                         sem.at[slot], fn)

    def fetch(blk):
        ys_scr[blk % 2] = jnp.zeros((ns, d), BF16)
        slabs(blk, lambda cp: cp.start())

    @pl.when(i == 0)
    def _():
        fetch(i)

    @pl.when(i + 1 < n_blk)
    def _():
        fetch(i + 1)

    slabs(i, lambda cp: cp.wait())

    srow = lax.broadcasted_iota(jnp.int32, (tm, ns), 1).astype(F32)
    weights = jnp.where(srow == info_ref[:, 0:1], info_ref[:, 2:3],
                        jnp.where(srow == info_ref[:, 1:2], info_ref[:, 3:4], 0.0))
    y = _dot(weights.astype(BF16), ys_scr[i % 2])
    x4 = x_ref[...] + mod_ref[0, 5:6, :] * y
    o_ref[...] = x4 * lax.rsqrt(jnp.mean(x4 * x4, axis=-1, keepdims=True) + RMS_EPS) * fg_ref[...]


def _combine(tbl, x3, mod, info, final_g, yr, seq, tm, ns, n_experts):
    t, d = x3.shape
    kern = functools.partial(_combine_kernel, n_experts=n_experts)
    return pl.pallas_call(
        kern,
        grid_spec=pltpu.PrefetchScalarGridSpec(
            num_scalar_prefetch=1,
            grid=(t // tm,),
            in_specs=[
                pl.BlockSpec((tm, d), lambda i, tb: (i, 0)),
                pl.BlockSpec((1, 6, d), _row_block(tm, seq)),
                pl.BlockSpec((tm, LANES), lambda i, tb: (i, 0)),
                pl.BlockSpec((1, d), lambda i, tb: (0, 0)),
                pl.BlockSpec(memory_space=pl.ANY),
            ],
            out_specs=pl.BlockSpec((tm, d), lambda i, tb: (i, 0)),
            scratch_shapes=[pltpu.VMEM((2, ns, d), BF16), pltpu.SemaphoreType.DMA((2,))],
        ),
        out_shape=jax.ShapeDtypeStruct((t, d), F32),
        compiler_params=_cparams(("arbitrary",), 40),
        name="moe_combine_final",
    )(tbl, x3, mod, info, final_g.reshape(1, d), yr)


def _moe(x3, mod, hb, info, tbl, w1, w3, w2, final_g, seq, tm):
    t, d = x3.shape
    n_experts = w1.shape[0]
    n_blk = t // tm
    ns = -(-(TOP_K * tm + n_experts * (ROW_ALIGN - 1)) // LANES) * LANES
    steps = (TOP_K * t + n_experts * n_blk * (ROW_ALIGN - 1)) // MOE_PAD + n_experts
    end = tbl[-1, 1, :n_experts] + tbl[-1, 0, :n_experts]
    nb = (end + MOE_PAD - 1) // MOE_PAD
    cum = jnp.cumsum(nb)
    n_active = cum[-1:].astype(jnp.int32)
    step = jnp.arange(steps, dtype=jnp.int32)
    owner = jnp.sum(jnp.minimum(step, n_active[0] - 1)[:, None] >= cum[None, :], axis=1)
    blk_expert = jnp.minimum(owner, n_experts - 1).astype(jnp.int32)
    flat = jnp.concatenate([tbl[:, :3, :n_experts].reshape(-1), (cum - nb) * MOE_PAD, n_active * MOE_PAD])
    flat = flat.astype(jnp.int32)
    xr = _dispatch(flat, hb, info, tm, ns, steps * MOE_PAD, n_experts)
    yr = _experts(blk_expert, n_active, xr, w1, w3, w2, w1.shape[2] // 2)
    return _combine(flat, x3, mod, info, final_g, yr, seq, tm, ns, n_experts)


def kernel(x, c, ada_w, ada_b, norm_mix_g, norm_ffn_g, final_g, ab_w_in, sgu_v_g, sgu_w_s, sgu_b_s, s5_lam_re, s5_lam_im, s5_log_dt, s5_b_re, s5_b_im, s5_c_re, s5_c_im, s5_d, s5_w_glu, ab_w_out, ffn_w1, ffn_w3, ffn_w2, mla_w_in, mla_q_norm_g, mla_kv_norm_g, mla_w_uq, mla_w_ukv, mla_w_o, moe_w_router, moe_w1, moe_w3, moe_w2):
    bn, seq, d = x.shape
    t = bn * seq
    tm = min(512, seq)
    x2 = x.reshape(t, d)
    mod = _adaln(c, ada_w, ada_b)

    a_out, u_tiles = _l0_in(x2, mod[0], norm_mix_g[0], ab_w_in[0], sgu_w_s[0], sgu_b_s[0], sgu_v_g[0], seq, tm)
    t_mat, ws, wy, aq = _s5_weights(s5_lam_re[0], s5_lam_im[0], s5_log_dt[0], s5_b_re[0], s5_b_im[0],
                                    s5_c_re[0], s5_c_im[0], S5_Q)
    y5 = _s5(u_tiles, t_mat, ws, wy, aq, bn, S5_Q)
    x2 = _l0_out(x2, mod[0], a_out, y5, u_tiles, s5_d[0], s5_w_glu[0], ab_w_out[0], seq, tm)
    x2 = _ffn(x2, mod[0], norm_ffn_g[0], ffn_w1[0], ffn_w3[0], ffn_w2[0], seq, tm)

    qt, k, vt = _mla_proj(x2, mod[1], norm_mix_g[1], mla_w_in[0], mla_q_norm_g[0], mla_kv_norm_g[0],
                          mla_w_uq[0], mla_w_ukv[0], seq, tm)
    o, w1b, w3b, w2b = _attention(qt, k, vt, bn, seq, min(1024, seq), min(1024, seq // 2),
                                  moe_w1[0], moe_w3[0], moe_w2[0])
    x3, hb, info, tbl = _attn_out(x2, mod[1], o, mla_w_o[0], norm_ffn_g[1], moe_w_router[0], seq, tm)
    out = _moe(x3, mod[1], hb, info, tbl, w1b, w3b, w2b, final_g, seq, tm)
    return out.reshape(bn, seq, d)
```
